```python
import math
import jax, jax.numpy as jnp
from jax import lax
import numpy as np

D_MODEL = 1024
BATCH = 1
SEQ = 16384
DEPTH = 1
DEC_BATCH = 4
DEC_SEQ = 4096
PAST_LEN = 128

MIX_WIDTH = D_MODEL
ATTN_WIDTH = D_MODEL // 2
SSM_WIDTH = MIX_WIDTH - ATTN_WIDTH
HEAD_DIM = 64
N_HEADS = ATTN_WIDTH // HEAD_DIM
DILATION_BRANCHES = ((128, 1), (512, 4), (2048, 16))
Q_BLOCK = 128
ROPE_THETA = 10000.0
SSM_GROUP = 16
N_SSM_GROUPS = SSM_WIDTH // SSM_GROUP
SSM_STATE = 64
FFN_HIDDEN = -(-8 * D_MODEL // (3 * 256)) * 256
PROJ_WIDTH = 3 * ATTN_WIDTH + SSM_WIDTH
N_MOD = 6
EPS = 1e-6

kernel_name = "hymba_dilated_s5_adaln_encoder"


def _rmsnorm(x, g):
    x32 = x.astype(jnp.float32)
    y = x32 * lax.rsqrt(jnp.mean(x32 * x32, axis=-1, keepdims=True) + EPS)
    return (y * g.astype(jnp.float32)).astype(x.dtype)


def _rope_tables(seq_len):
    inv = 1.0 / (ROPE_THETA ** (jnp.arange(0, HEAD_DIM, 2, dtype=jnp.float32) / HEAD_DIM))
    ang = jnp.arange(seq_len, dtype=jnp.float32)[:, None] * inv[None, :]
    return jnp.cos(ang), jnp.sin(ang)


def _rope(x, cos, sin):
    x32 = x.astype(jnp.float32)
    x1, x2 = jnp.split(x32, 2, axis=-1)
    c = cos[None, :, None, :]
    s = sin[None, :, None, :]
    return jnp.concatenate([x1 * c - x2 * s, x2 * c + x1 * s], axis=-1).astype(x.dtype)


def _dilated_attention(q, k, v):
    b, s_len, h, dh = q.shape
    n_blk = s_len // Q_BLOCK
    scale = HEAD_DIM ** -0.5
    branches = []
    for window, dil in DILATION_BRANCHES:
        half = (window // (2 * dil)) * dil
        pads = ((0, 0), (half, half), (0, 0), (0, 0))
        branches.append((jnp.pad(k, pads), jnp.pad(v, pads), half, dil))

    def block(i):
        start = i * Q_BLOCK
        qb = lax.dynamic_slice_in_dim(q, start, Q_BLOCK, axis=1)
        qpos = start + jnp.arange(Q_BLOCK)
        outs, lses = [], []
        for kp, vp, half, dil in branches:
            offs = jnp.arange(-half, half + 1, dil)
            pos = qpos[:, None] + offs[None, :]
            valid = (pos >= 0) & (pos < s_len)
            kg = jnp.take(kp, pos + half, axis=1)
            vg = jnp.take(vp, pos + half, axis=1)
            sc = jnp.einsum('bqhd,bqnhd->bhqn', qb, kg,
                            preferred_element_type=jnp.float32) * scale
            sc = jnp.where(valid[None, None], sc, -jnp.inf)
            m = jnp.max(sc, axis=-1, keepdims=True)
            p = jnp.exp(sc - m)
            den = jnp.sum(p, axis=-1, keepdims=True)
            o = jnp.einsum('bhqn,bqnhd->bqhd', p / den, vg.astype(jnp.float32))
            outs.append(o)
            lses.append(jnp.transpose((m + jnp.log(den))[..., 0], (0, 2, 1)))
        wts = jax.nn.softmax(jnp.stack(lses, axis=0), axis=0)
        out = jnp.sum(wts[..., None] * jnp.stack(outs, axis=0), axis=0)
        return out.astype(q.dtype)

    out = lax.map(block, jnp.arange(n_blk))
    return jnp.moveaxis(out, 0, 1).reshape(b, s_len, h * dh)


def _complex_linear_combine(e1, e2):
    a1r, a1i, b1r, b1i = e1
    a2r, a2i, b2r, b2i = e2
    return (a2r * a1r - a2i * a1i,
            a2r * a1i + a2i * a1r,
            a2r * b1r - a2i * b1i + b2r,
            a2r * b1i + a2i * b1r + b2i)


def _s5_direction(ug, lam_re, lam_im, log_dt, b_re, b_im, c_re, c_im, reverse):
    dt = jnp.exp(log_dt)[:, None]
    mag = jnp.exp(lam_re * dt)
    ang = lam_im * dt
    a_re = mag * jnp.cos(ang)
    a_im = mag * jnp.sin(ang)
    nr = a_re - 1.0
    den = lam_re * lam_re + lam_im * lam_im
    z_re = ((nr * lam_re + a_im * lam_im) / den)[..., None]
    z_im = ((a_im * lam_re - nr * lam_im) / den)[..., None]
    bb_re = z_re * b_re - z_im * b_im
    bb_im = z_re * b_im + z_im * b_re
    bu_re = jnp.einsum('bsgc,gpc->bsgp', ug, bb_re)
    bu_im = jnp.einsum('bsgc,gpc->bsgp', ug, bb_im)
    shape = bu_re.shape
    elems = (jnp.broadcast_to(a_re, shape), jnp.broadcast_to(a_im, shape), bu_re, bu_im)
    _, _, h_re, h_im = lax.associative_scan(_complex_linear_combine, elems,
                                            reverse=reverse, axis=1)
    return (jnp.einsum('bsgp,gcp->bsgc', h_re, c_re)
            - jnp.einsum('bsgp,gcp->bsgc', h_im, c_im))


def _s5_bidirectional(u, lam_re, lam_im, log_dt, b_re, b_im, c_re, c_im, d_skip, w_glu, b_glu):
    b, s_len, _ = u.shape
    ug = u.reshape(b, s_len, N_SSM_GROUPS, SSM_GROUP)
    y_f = _s5_direction(ug, lam_re[0], lam_im[0], log_dt[0], b_re[0], b_im[0],
                        c_re[0], c_im[0], reverse=False)
    y_b = _s5_direction(ug, lam_re[1], lam_im[1], log_dt[1], b_re[1], b_im[1],
                        c_re[1], c_im[1], reverse=True)
    y = (y_f + y_b).reshape(b, s_len, SSM_WIDTH) + d_skip * u
    g = jax.nn.gelu(y)
    return g * jax.nn.sigmoid(g @ w_glu + b_glu)


def _encode(x, c, w_ada, b_ada, norm1_g, w_in, lam_re, lam_im, log_dt, b_re, b_im,
            c_re, c_im, d_skip, w_glu, b_glu, attn_norm_g, ssm_norm_g, w_o,
            norm2_g, w1, w3, w2, final_g):
    b, s_len, _ = x.shape
    cos, sin = _rope_tables(s_len)
    for l in range(DEPTH):
        mod = (jax.nn.silu(c) @ w_ada[l] + b_ada[l])[:, None, :]
        sh1, sc1, g1, sh2, sc2, g2 = jnp.split(mod, N_MOD, axis=-1)
        h = _rmsnorm(x, norm1_g[l]) * (1.0 + sc1) + sh1
        proj = h @ w_in[l]
        q, k, v, u = jnp.split(proj, [ATTN_WIDTH, 2 * ATTN_WIDTH, 3 * ATTN_WIDTH], axis=-1)
        q = _rope(q.reshape(b, s_len, N_HEADS, HEAD_DIM), cos, sin)
        k = _rope(k.reshape(b, s_len, N_HEADS, HEAD_DIM), cos, sin)
        v = v.reshape(b, s_len, N_HEADS, HEAD_DIM)
        attn = _dilated_attention(q, k, v)
        ssm = _s5_bidirectional(u, lam_re[l], lam_im[l], log_dt[l], b_re[l], b_im[l],
                                c_re[l], c_im[l], d_skip[l], w_glu[l], b_glu[l])
        mixed = jnp.concatenate([_rmsnorm(attn, attn_norm_g[l]),
                                 _rmsnorm(ssm, ssm_norm_g[l])], axis=-1) @ w_o[l]
        x = x + g1 * mixed
        h = _rmsnorm(x, norm2_g[l]) * (1.0 + sc2) + sh2
        ffn = (jax.nn.silu(h @ w1[l]) * (h @ w3[l])) @ w2[l]
        x = x + g2 * ffn
    return _rmsnorm(x, final_g)


def setup_inputs(seed: int = 0) -> dict:
    key = jax.random.key(seed)
    ks = jax.random.split(key, 32)
    f32 = jnp.float32
    L, D, G, P, C = DEPTH, D_MODEL, N_SSM_GROUPS, SSM_STATE, SSM_GROUP

    def nrm(k, shape, scale):
        return jax.random.normal(k, shape, f32) * scale

    lam_im0 = jnp.broadcast_to(jnp.pi * jnp.arange(P, dtype=f32), (L, 2, G, P))
    return {
        "x_prompt": nrm(ks[0], (BATCH, SEQ, D), 1.0),
        "x_sample": nrm(ks[1], (DEC_BATCH, DEC_SEQ, D), 1.0),
        "c_prompt": nrm(ks[2], (BATCH, D), 1.0),
        "c_sample": nrm(ks[3], (DEC_BATCH, D), 1.0),
        "w_ada": nrm(ks[4], (L, D, N_MOD * D), 0.5 * D ** -0.5),
        "b_ada": nrm(ks[5], (L, N_MOD * D), 0.01),
        "norm1_g": 1.0 + nrm(ks[6], (L, D), 0.02),
        "w_in": nrm(ks[7], (L, D, PROJ_WIDTH), D ** -0.5),
        "lam_re": -0.5 + nrm(ks[8], (L, 2, G, P), 0.01),
        "lam_im": lam_im0 + nrm(ks[9], (L, 2, G, P), 0.01),
        "log_dt": jax.random.uniform(ks[10], (L, 2, G), f32, math.log(0.001), math.log(0.1)),
        "b_re": nrm(ks[11], (L, 2, G, P, C), (2 * C) ** -0.5),
        "b_im": nrm(ks[12], (L, 2, G, P, C), (2 * C) ** -0.5),
        "c_re": nrm(ks[13], (L, 2, G, C, P), (2 * P) ** -0.5),
        "c_im": nrm(ks[14], (L, 2, G, C, P), (2 * P) ** -0.5),
        "d_skip": nrm(ks[15], (L, SSM_WIDTH), 1.0),
        "w_glu": nrm(ks[16], (L, SSM_WIDTH, SSM_WIDTH), SSM_WIDTH ** -0.5),
        "b_glu": nrm(ks[17], (L, SSM_WIDTH), 0.01),
        "attn_norm_g": 1.0 + nrm(ks[18], (L, ATTN_WIDTH), 0.02),
        "ssm_norm_g": 1.0 + nrm(ks[19], (L, SSM_WIDTH), 0.02),
        "w_o": nrm(ks[20], (L, MIX_WIDTH, D), MIX_WIDTH ** -0.5),
        "norm2_g": 1.0 + nrm(ks[21], (L, D), 0.02),
        "w1": nrm(ks[22], (L, D, FFN_HIDDEN), D ** -0.5),
        "w3": nrm(ks[23], (L, D, FFN_HIDDEN), D ** -0.5),
        "w2": nrm(ks[24], (L, FFN_HIDDEN, D), FFN_HIDDEN ** -0.5),
        "final_g": 1.0 + nrm(ks[25], (D,), 0.02),
    }


def reference(x_prompt, x_sample, c_prompt, c_sample, w_ada, b_ada, norm1_g, w_in,
              lam_re, lam_im, log_dt, b_re, b_im, c_re, c_im, d_skip, w_glu, b_glu,
              attn_norm_g, ssm_norm_g, w_o, norm2_g, w1, w3, w2, final_g):
    y_prompt = _encode(x_prompt, c_prompt, w_ada, b_ada, norm1_g, w_in, lam_re, lam_im,
                       log_dt, b_re, b_im, c_re, c_im, d_skip, w_glu, b_glu,
                       attn_norm_g, ssm_norm_g, w_o, norm2_g, w1, w3, w2, final_g)
    y_sample = _encode(x_sample, c_sample, w_ada, b_ada, norm1_g, w_in, lam_re, lam_im,
                       log_dt, b_re, b_im, c_re, c_im, d_skip, w_glu, b_glu,
                       attn_norm_g, ssm_norm_g, w_o, norm2_g, w1, w3, w2, final_g)
    return (y_prompt, y_sample)
```

```python
import functools
import math

import jax
import jax.numpy as jnp
from jax import lax
from jax.experimental import pallas as pl
from jax.experimental.pallas import tpu as pltpu

F32 = jnp.float32
BF16 = jnp.bfloat16

D_MODEL = 1024
ATTN_WIDTH = 512
SSM_WIDTH = 512
HEAD_DIM = 64
PAIR = 2 * HEAD_DIM
N_PAIRS = ATTN_WIDTH // PAIR
DILATIONS = (1, 4, 16)
HALF_KEYS = 64
Q_TILE = 128
K_WIN = Q_TILE + 2 * HALF_KEYS
ROPE_THETA = 10000.0
SSM_GROUP = 16
N_GROUPS = SSM_WIDTH // SSM_GROUP
SSM_STATE = 64
CHUNK = 16
BUNDLE_GROUPS = 8
N_BUNDLES = N_GROUPS // BUNDLE_GROUPS
BUNDLE_W = CHUNK * BUNDLE_GROUPS * SSM_GROUP
STATE_W = BUNDLE_GROUPS * SSM_STATE
FFN_HIDDEN = 2816
N_MOD = 6
EPS = 1e-6
NEG_BIG = -1e30
VMEM_LIMIT = 56 * 1024 * 1024


def _cparams(sem):
    return pltpu.CompilerParams(dimension_semantics=sem, vmem_limit_bytes=VMEM_LIMIT)


def _const_spec(shape):
    nd = len(shape)
    return pl.BlockSpec(shape, lambda *_: (0,) * nd, pipeline_mode=pl.Buffered(1))


def _sigmoid(x):
    return 1.0 / (1.0 + jnp.exp(-x))


def _rms(x, g):
    return x * lax.rsqrt(jnp.mean(x * x, axis=-1, keepdims=True) + EPS) * g


def _mod_kernel(c_ref, w_ref, b_ref, o_ref):
    c = c_ref[...]
    s = c * _sigmoid(c)
    o_ref[...] = jnp.dot(s.astype(BF16), w_ref[...].astype(BF16),
                         preferred_element_type=F32) + b_ref[...]


def _mod_call(c_all, w_ada, b_ada):
    rows = c_all.shape[0]
    n_out = w_ada.shape[1]
    tn = 1024
    return pl.pallas_call(
        _mod_kernel,
        grid=(n_out // tn,),
        in_specs=[pl.BlockSpec((rows, D_MODEL), lambda j: (0, 0)),
                  pl.BlockSpec((D_MODEL, tn), lambda j: (0, j)),
                  pl.BlockSpec((1, tn), lambda j: (0, j))],
        out_specs=pl.BlockSpec((rows, tn), lambda j: (0, j)),
        out_shape=jax.ShapeDtypeStruct((rows, n_out), F32),
        compiler_params=_cparams(("arbitrary",)),
        name="adaln_mod",
    )(c_all, w_ada, b_ada)


def _proj_kernel(x_ref, mod_ref, g_ref, w_ref, cos_ref, sin_ref,
                 q_ref, k_ref, v_ref, u_ref):
    x = x_ref[...]
    h = _rms(x, g_ref[...]) * (1.0 + mod_ref[1:2, :]) + mod_ref[0:1, :]
    p = jnp.dot(h.astype(BF16), w_ref[...], preferred_element_type=F32)
    cos = jnp.concatenate([cos_ref[...]] * N_PAIRS, axis=1)
    sin = jnp.concatenate([sin_ref[...]] * N_PAIRS, axis=1)
    lane = lax.broadcasted_iota(jnp.int32, (1, ATTN_WIDTH), 1)
    first_half = (lane % HEAD_DIM) < (HEAD_DIM // 2)

    def rope(t):
        fwd = pltpu.roll(t, ATTN_WIDTH - HEAD_DIM // 2, 1)
        bwd = pltpu.roll(t, HEAD_DIM // 2, 1)
        return t * cos + jnp.where(first_half, fwd, bwd) * sin

    q = rope(p[:, 0:ATTN_WIDTH]) * (HEAD_DIM ** -0.5)
    k = rope(p[:, ATTN_WIDTH:2 * ATTN_WIDTH])
    q_ref[...] = q.astype(BF16)
    k_ref[...] = k.astype(BF16)
    v_ref[...] = p[:, 2 * ATTN_WIDTH:3 * ATTN_WIDTH].astype(BF16)
    u_ref[...] = p[:, 3 * ATTN_WIDTH:]


def _proj_call(x, mod3, norm1_g, w_in, cos_t, sin_t, tm):
    b, s, _ = x.shape
    row = lambda bi, i: (bi, i, 0)
    out_sd = lambda w, dt: jax.ShapeDtypeStruct((b, s, w), dt)
    return pl.pallas_call(
        _proj_kernel,
        grid=(b, s // tm),
        in_specs=[pl.BlockSpec((None, tm, D_MODEL), row),
                  pl.BlockSpec((None, N_MOD, D_MODEL), lambda bi, i: (bi, 0, 0)),
                  _const_spec((1, D_MODEL)),
                  _const_spec((D_MODEL, 4 * ATTN_WIDTH)),
                  pl.BlockSpec((tm, PAIR), lambda bi, i: (i, 0)),
                  pl.BlockSpec((tm, PAIR), lambda bi, i: (i, 0))],
        out_specs=[pl.BlockSpec((None, tm, ATTN_WIDTH), row)] * 4,
        out_shape=[out_sd(ATTN_WIDTH, BF16), out_sd(ATTN_WIDTH, BF16),
                   out_sd(ATTN_WIDTH, BF16), out_sd(SSM_WIDTH, F32)],
        compiler_params=_cparams(("arbitrary", "arbitrary")),
        name="proj_rope",
    )(x, mod3, norm1_g, w_in, cos_t, sin_t)


def _attn_kernel(q_ref, k_ref, v_ref, o_ref, lse_ref, *, sub_len, tq):
    i = pl.program_id(3)
    lane_b = lax.broadcasted_iota(jnp.int32, (Q_TILE, PAIR), 1) < HEAD_DIM
    row = lax.broadcasted_iota(jnp.int32, (2 * Q_TILE, K_WIN), 0) % Q_TILE
    col = lax.broadcasted_iota(jnp.int32, (2 * Q_TILE, K_WIN), 1)
    delta0 = col - row
    for sb in range(tq // Q_TILE):
        m0 = i * tq + sb * Q_TILE
        start = jnp.clip(m0 - HALF_KEYS, 0, sub_len - K_WIN)
        start = pl.multiple_of(start, HALF_KEYS)
        q = q_ref[sb * Q_TILE:(sb + 1) * Q_TILE, :]
        kw = k_ref[pl.ds(start, K_WIN), :]
        vw = v_ref[pl.ds(start, K_WIN), :]
        zero = jnp.zeros_like(q)
        q2 = jnp.concatenate([jnp.where(lane_b, q, zero), jnp.where(lane_b, zero, q)], axis=0)
        s = lax.dot_general(q2, kw, (((1,), (1,)), ((), ())), preferred_element_type=F32)
        valid = jnp.abs(delta0 + (start - m0)) <= HALF_KEYS
        s = jnp.where(valid, s, NEG_BIG)
        m = jnp.max(s, axis=1, keepdims=True)
        p = jnp.exp(s - m)
        l = jnp.sum(p, axis=1, keepdims=True)
        pv = jnp.dot(p.astype(BF16), vw, preferred_element_type=F32)
        o2 = pv / l
        lse2 = jnp.broadcast_to(m + jnp.log(l), (2 * Q_TILE, PAIR))
        o_ref[sb * Q_TILE:(sb + 1) * Q_TILE, :] = jnp.where(lane_b, o2[:Q_TILE], o2[Q_TILE:])
        lse_ref[sb * Q_TILE:(sb + 1) * Q_TILE, :] = jnp.where(lane_b, lse2[:Q_TILE], lse2[Q_TILE:])


def _attn_call(qd, kd, vd):
    b, d, sub_len, _ = qd.shape
    tq = min(1024, sub_len)
    qspec = pl.BlockSpec((None, None, tq, PAIR), lambda bi, r, hp, i: (bi, r, i, hp))
    kspec = pl.BlockSpec((None, None, sub_len, PAIR), lambda bi, r, hp, i: (bi, r, 0, hp))
    out_sd = jax.ShapeDtypeStruct((b, d, sub_len, ATTN_WIDTH), F32)
    return pl.pallas_call(
        functools.partial(_attn_kernel, sub_len=sub_len, tq=tq),
        grid=(b, d, N_PAIRS, sub_len // tq),
        in_specs=[qspec, kspec, kspec],
        out_specs=[qspec, qspec],
        out_shape=[out_sd, out_sd],
        compiler_params=_cparams(("arbitrary",) * 4),
        name=f"dilated_attn_d{d}",
    )(qd, kd, vd)


def _ssm_in_kernel(x_ref, t_ref, we_ref, y_ref, e_ref):
    x = x_ref[...]
    y_ref[...] = jnp.dot(x, t_ref[...], preferred_element_type=F32)
    e_ref[...] = jnp.dot(x, we_ref[...], preferred_element_type=F32)


def _ssm_in_call(xb, t_w, we_w, mc):
    b, _, n_chunks, _ = xb.shape
    tile = pl.BlockSpec((None, None, mc, BUNDLE_W), lambda bun, bi, i: (bi, bun, i, 0))
    wspec = pl.BlockSpec((None, BUNDLE_W, BUNDLE_W), lambda bun, bi, i: (bun, 0, 0),
                         pipeline_mode=pl.Buffered(1))
    out_sd = jax.ShapeDtypeStruct((b, N_BUNDLES, n_chunks, BUNDLE_W), F32)
    return pl.pallas_call(
        _ssm_in_kernel,
        grid=(N_BUNDLES, b, n_chunks // mc),
        in_specs=[tile, wspec, wspec],
        out_specs=[tile, tile],
        out_shape=[out_sd, out_sd],
        compiler_params=_cparams(("arbitrary",) * 3),
        name="s5_chunk_in",
    )(xb, t_w, we_w)


def _ssm_scan_kernel(e_ref, a_ref, h_ref, *, n_chunks):
    rev = pl.program_id(2) == 1
    ar = a_ref[0:1, :]
    ai = a_ref[1:2, :]

    def body(s, carry):
        hr, hi = carry
        k = jnp.where(rev, n_chunks - 1 - s, s)
        h_ref[pl.ds(k, 1), 0:STATE_W] = hr
        h_ref[pl.ds(k, 1), STATE_W:2 * STATE_W] = hi
        er = e_ref[pl.ds(k, 1), 0:STATE_W]
        ei = e_ref[pl.ds(k, 1), STATE_W:2 * STATE_W]
        return ar * hr - ai * hi + er, ar * hi + ai * hr + ei

    zero = jnp.zeros((1, STATE_W), F32)
    lax.fori_loop(0, n_chunks, body, (zero, zero))


def _ssm_scan_call(e, a_pow):
    b, _, n_chunks, _ = e.shape
    tile = pl.BlockSpec((None, None, n_chunks, 2 * STATE_W), lambda bi, bun, dr: (bi, bun, 0, dr))
    return pl.pallas_call(
        functools.partial(_ssm_scan_kernel, n_chunks=n_chunks),
        grid=(b, N_BUNDLES, 2),
        in_specs=[tile,
                  pl.BlockSpec((None, None, 2, STATE_W), lambda bi, bun, dr: (bun, dr, 0, 0))],
        out_specs=tile,
        out_shape=jax.ShapeDtypeStruct(e.shape, F32),
        compiler_params=_cparams(("arbitrary",) * 3),
        name="s5_chunk_scan",
    )(e, a_pow)


def _ssm_out_kernel(h_ref, v_ref, yi_ref, y_ref):
    y_ref[...] = yi_ref[...] + jnp.dot(h_ref[...].astype(BF16), v_ref[...],
                                       preferred_element_type=F32)


def _ssm_out_call(h_in, v_w, y_intra, mc):
    b, _, n_chunks, _ = h_in.shape
    tile = pl.BlockSpec((None, None, mc, BUNDLE_W), lambda bun, bi, i: (bi, bun, i, 0))
    wspec = pl.BlockSpec((None, BUNDLE_W, BUNDLE_W), lambda bun, bi, i: (bun, 0, 0),
                         pipeline_mode=pl.Buffered(1))
    return pl.pallas_call(
        _ssm_out_kernel,
        grid=(N_BUNDLES, b, n_chunks // mc),
        in_specs=[tile, wspec, tile],
        out_specs=tile,
        out_shape=jax.ShapeDtypeStruct(h_in.shape, F32),
        compiler_params=_cparams(("arbitrary",) * 3),
        name="s5_chunk_out",
    )(h_in, v_w, y_intra)


def _cpow(lam_re, lam_im, dt, n):
    nn = n.astype(F32).reshape(n.shape + (1, 1))
    mag = jnp.exp(nn * (lam_re * dt))
    ang = nn * (lam_im * dt)
    return mag * jnp.cos(ang), mag * jnp.sin(ang)


def _ssm_weights(lam_re, lam_im, log_dt, b_re, b_im, c_re, c_im):
    hi = lax.Precision.HIGHEST
    steps = jnp.arange(CHUNK)
    eye = jnp.eye(BUNDLE_GROUPS, dtype=F32)
    k_lag, w_e, v_c, a_pow = [], [], [], []
    for dr in range(2):
        lr, li = lam_re[dr], lam_im[dr]
        dt = jnp.exp(log_dt[dr])[:, None]
        a_r, a_i = _cpow(lr, li, dt, jnp.ones((), F32))
        nr = a_r - 1.0
        den = lr * lr + li * li
        z_r = ((nr * lr + a_i * li) / den)[..., None]
        z_i = ((a_i * lr - nr * li) / den)[..., None]
        bb_r = z_r * b_re[dr] - z_i * b_im[dr]
        bb_i = z_r * b_im[dr] + z_i * b_re[dr]
        cr = jnp.swapaxes(c_re[dr], 1, 2)
        ci = jnp.swapaxes(c_im[dr], 1, 2)

        p_r, p_i = _cpow(lr, li, dt, steps)
        ca_r = p_r[..., None] * cr - p_i[..., None] * ci
        ca_i = p_r[..., None] * ci + p_i[..., None] * cr
        k_lag.append(jnp.einsum('lgpc,gpd->lgdc', ca_r, bb_r, precision=hi)
                     - jnp.einsum('lgpc,gpd->lgdc', ca_i, bb_i, precision=hi))

        e_r, e_i = _cpow(lr, li, dt, (CHUNK - 1 - steps) if dr == 0 else steps)
        we_r = e_r[..., None] * bb_r - e_i[..., None] * bb_i
        we_i = e_r[..., None] * bb_i + e_i[..., None] * bb_r
        w_e.append((we_r, we_i))

        o_r, o_i = _cpow(lr, li, dt, (steps + 1) if dr == 0 else (CHUNK - steps))
        vo_r = o_r[..., None] * cr - o_i[..., None] * ci
        vo_i = o_r[..., None] * ci + o_i[..., None] * cr
        v_c.append((vo_r, -vo_i))

        a_pow.append(_cpow(lr, li, dt, jnp.full((), CHUNK, F32)))

    jj = steps[:, None]
    tt = steps[None, :]
    kf = k_lag[0][jnp.clip(tt - jj, 0, CHUNK - 1)] * (tt >= jj)[..., None, None, None].astype(F32)
    kb = k_lag[1][jnp.clip(jj - tt, 0, CHUNK - 1)] * (jj >= tt)[..., None, None, None].astype(F32)
    toe = (kf + kb).reshape(CHUNK, CHUNK, N_BUNDLES, BUNDLE_GROUPS, SSM_GROUP, SSM_GROUP)
    toe = jnp.transpose(toe, (2, 0, 3, 4, 1, 5))
    t_w = toe[:, :, :, :, :, None, :] * eye[None, None, :, None, None, :, None]
    t_w = t_w.reshape(N_BUNDLES, BUNDLE_W, BUNDLE_W).astype(BF16)

    def bundle(x):
        return x.reshape((x.shape[0], N_BUNDLES, BUNDLE_GROUPS) + x.shape[2:])

    quarters = [w_e[0][0], w_e[0][1], w_e[1][0], w_e[1][1]]
    we = jnp.stack([bundle(x) for x in quarters], axis=0)
    we = jnp.transpose(we, (2, 1, 3, 5, 0, 4))
    we = we[:, :, :, :, :, None, :] * eye[None, None, :, None, None, :, None]
    we_w = we.reshape(N_BUNDLES, BUNDLE_W, 4 * STATE_W).astype(BF16)

    quarters = [v_c[0][0], v_c[0][1], v_c[1][0], v_c[1][1]]
    vv = jnp.stack([bundle(x) for x in quarters], axis=0)
    vv = jnp.transpose(vv, (2, 0, 3, 4, 1, 5))
    vv = vv[:, :, :, :, :, None, :] * eye[None, None, :, None, None, :, None]
    v_w = vv.reshape(N_BUNDLES, 4 * STATE_W, BUNDLE_W).astype(BF16)

    ap = jnp.stack([jnp.stack([a_pow[dr][0], a_pow[dr][1]], axis=0) for dr in range(2)], axis=0)
    ap = ap.reshape(2, 2, N_BUNDLES, STATE_W)
    ap = jnp.transpose(ap, (2, 0, 1, 3))
    return t_w, we_w, v_w, ap


def _post_kernel(x_ref, mod_ref, o1_ref, o2_ref, o3_ref, l1_ref, l2_ref, l3_ref,
                 y_ref, u_ref, dskip_ref, wglu_ref, bglu_ref, ang_ref, sng_ref,
                 wo_ref, n2g_ref, w1_ref, w3_ref, w2_ref, fg_ref, out_ref):
    l1, l2, l3 = l1_ref[...], l2_ref[...], l3_ref[...]
    lm = jnp.maximum(jnp.maximum(l1, l2), l3)
    e1, e2, e3 = jnp.exp(l1 - lm), jnp.exp(l2 - lm), jnp.exp(l3 - lm)
    attn = (e1 * o1_ref[...] + e2 * o2_ref[...] + e3 * o3_ref[...]) / (e1 + e2 + e3)
    an = _rms(attn, ang_ref[...])

    y = y_ref[...] + dskip_ref[...] * u_ref[...]
    g = 0.5 * y * (1.0 + jnp.tanh(math.sqrt(2.0 / math.pi) * (y + 0.044715 * (y * y * y))))
    z = jnp.dot(g.astype(BF16), wglu_ref[...], preferred_element_type=F32) + bglu_ref[...]
    sn = _rms(g * _sigmoid(z), sng_ref[...])

    mixed = (jnp.dot(an.astype(BF16), wo_ref[0:ATTN_WIDTH, :], preferred_element_type=F32)
             + jnp.dot(sn.astype(BF16), wo_ref[ATTN_WIDTH:, :], preferred_element_type=F32))
    x1 = x_ref[...] + mod_ref[2:3, :] * mixed

    h = (_rms(x1, n2g_ref[...]) * (1.0 + mod_ref[4:5, :]) + mod_ref[3:4, :]).astype(BF16)
    a = jnp.dot(h, w1_ref[...], preferred_element_type=F32)
    bgate = jnp.dot(h, w3_ref[...], preferred_element_type=F32)
    hid = (a * _sigmoid(a) * bgate).astype(BF16)
    ffn = jnp.dot(hid, w2_ref[...], preferred_element_type=F32)
    x2 = x1 + mod_ref[5:6, :] * ffn
    out_ref[...] = _rms(x2, fg_ref[...])


def _post_call(x, mod3, outs, lses, y_ssm, u, d_skip, w_glu, b_glu, attn_g, ssm_g,
               w_o, norm2_g, w1, w3, w2, final_g, tm):
    b, s, _ = x.shape
    row = lambda bi, i: (bi, i, 0)
    wide = pl.BlockSpec((None, tm, D_MODEL), row)
    half = pl.BlockSpec((None, tm, ATTN_WIDTH), row)
    return pl.pallas_call(
        _post_kernel,
        grid=(b, s // tm),
        in_specs=[wide, pl.BlockSpec((None, N_MOD, D_MODEL), lambda bi, i: (bi, 0, 0)),
                  half, half, half, half, half, half, half, half,
                  _const_spec((1, SSM_WIDTH)), _const_spec((SSM_WIDTH, SSM_WIDTH)),
                  _const_spec((1, SSM_WIDTH)), _const_spec((1, ATTN_WIDTH)),
                  _const_spec((1, SSM_WIDTH)), _const_spec((D_MODEL, D_MODEL)),
                  _const_spec((1, D_MODEL)), _const_spec((D_MODEL, FFN_HIDDEN)),
                  _const_spec((D_MODEL, FFN_HIDDEN)), _const_spec((FFN_HIDDEN, D_MODEL)),
                  _const_spec((1, D_MODEL))],
        out_specs=wide,
        out_shape=jax.ShapeDtypeStruct((b, s, D_MODEL), F32),
        compiler_params=_cparams(("arbitrary", "arbitrary")),
        name="merge_mix_ffn",
    )(x, mod3, *outs, *lses, y_ssm, u, d_skip, w_glu, b_glu, attn_g, ssm_g,
      w_o, norm2_g, w1, w3, w2, final_g)


def _rope_tables(seq_len):
    inv = 1.0 / (ROPE_THETA ** (jnp.arange(0, HEAD_DIM, 2, dtype=F32) / HEAD_DIM))
    ang = jnp.arange(seq_len, dtype=F32)[:, None] * inv[None, :]
    cos, sin = jnp.cos(ang), jnp.sin(ang)
    return (jnp.concatenate([cos, cos] * 2, axis=1),
            jnp.concatenate([-sin, sin] * 2, axis=1))


def _to_residue_major(t, d):
    b, s, w = t.shape
    return jnp.transpose(t.reshape(b, s // d, d, w), (0, 2, 1, 3))


def _from_residue_major(t):
    b, d, sub_len, w = t.shape
    return jnp.transpose(t, (0, 2, 1, 3)).reshape(b, d * sub_len, w)


def _encode(x, mod3, p):
    b, s, _ = x.shape
    n_chunks = s // CHUNK
    cos_t, sin_t = _rope_tables(s)
    q, k, v, u = _proj_call(x, mod3, p["norm1_g"], p["w_in"], cos_t, sin_t, tm=512)

    outs, lses = [], []
    for d in DILATIONS:
        o, lse = _attn_call(_to_residue_major(q, d), _to_residue_major(k, d),
                            _to_residue_major(v, d))
        outs.append(_from_residue_major(o))
        lses.append(_from_residue_major(lse))

    xb = u.astype(BF16).reshape(b, n_chunks, CHUNK, N_BUNDLES, PAIR)
    xb = jnp.transpose(xb, (0, 3, 1, 2, 4)).reshape(b, N_BUNDLES, n_chunks, BUNDLE_W)
    mc = min(256, n_chunks)
    y_intra, e = _ssm_in_call(xb, p["t_w"], p["we_w"], mc)
    h_in = _ssm_scan_call(e, p["a_pow"])
    yb = _ssm_out_call(h_in, p["v_w"], y_intra, mc)
    y_ssm = jnp.transpose(yb.reshape(b, N_BUNDLES, n_chunks, CHUNK, PAIR),
                          (0, 2, 3, 1, 4)).reshape(b, s, SSM_WIDTH)

    return _post_call(x, mod3, outs, lses, y_ssm, u, p["d_skip"], p["w_glu"], p["b_glu"],
                      p["attn_norm_g"], p["ssm_norm_g"], p["w_o"], p["norm2_g"],
                      p["w1"], p["w3"], p["w2"], p["final_g"], tm=256)


def kernel(x_prompt, x_sample, c_prompt, c_sample, w_ada, b_ada, norm1_g, w_in, lam_re, lam_im,
           log_dt, b_re, b_im, c_re, c_im, d_skip, w_glu, b_glu, attn_norm_g, ssm_norm_g, w_o,
           norm2_g, w1, w3, w2, final_g):
    nb_p, nb_s = c_prompt.shape[0], c_sample.shape[0]
    rows = -(-(nb_p + nb_s) // 8) * 8
    c_all = jnp.concatenate([c_prompt, c_sample,
                             jnp.zeros((rows - nb_p - nb_s, D_MODEL), F32)], axis=0)
    mod = _mod_call(c_all, w_ada[0], b_ada[0][None, :]).reshape(rows, N_MOD, D_MODEL)

    t_w, we_w, v_w, a_pow = _ssm_weights(lam_re[0], lam_im[0], log_dt[0], b_re[0], b_im[0],
                                         c_re[0], c_im[0])
    p = dict(norm1_g=norm1_g[0][None, :], w_in=w_in[0].astype(BF16),
             t_w=t_w, we_w=we_w, v_w=v_w, a_pow=a_pow,
             d_skip=d_skip[0][None, :], w_glu=w_glu[0].astype(BF16), b_glu=b_glu[0][None, :],
             attn_norm_g=attn_norm_g[0][None, :], ssm_norm_g=ssm_norm_g[0][None, :],
             w_o=w_o[0].astype(BF16), norm2_g=norm2_g[0][None, :],
             w1=w1[0].astype(BF16), w3=w3[0].astype(BF16), w2=w2[0].astype(BF16),
             final_g=final_g[None, :])
    y_prompt = _encode(x_prompt, mod[:nb_p], p)
    y_sample = _encode(x_sample, mod[nb_p:nb_p + nb_s], p)
    return (y_prompt, y_sample)
```

```python
import functools
import math

import jax
import jax.numpy as jnp
from jax import lax
from jax.experimental import pallas as pl
from jax.experimental.pallas import tpu as pltpu

F32 = jnp.float32
BF16 = jnp.bfloat16

D_MODEL = 1024
ATTN_WIDTH = 512
SSM_WIDTH = 512
HEAD_DIM = 64
PAIR = 2 * HEAD_DIM
N_PAIRS = ATTN_WIDTH // PAIR
DILATIONS = (1, 4, 16)
HALF_KEYS = 64
Q_TILE = 128
K_WIN = Q_TILE + 2 * HALF_KEYS
ROPE_THETA = 10000.0
SSM_GROUP = 16
N_GROUPS = SSM_WIDTH // SSM_GROUP
SSM_STATE = 64
CHUNK = 16
BUNDLE_GROUPS = 8
N_BUNDLES = N_GROUPS // BUNDLE_GROUPS
BUNDLE_W = CHUNK * BUNDLE_GROUPS * SSM_GROUP
STATE_W = BUNDLE_GROUPS * SSM_STATE
FFN_HIDDEN = 2816
N_MOD = 6
EPS = 1e-6
NEG_BIG = -1e30
VMEM_LIMIT = 56 * 1024 * 1024


def _cparams(sem):
    return pltpu.CompilerParams(dimension_semantics=sem, vmem_limit_bytes=VMEM_LIMIT)


def _const_spec(shape):
    nd = len(shape)
    return pl.BlockSpec(shape, lambda *_: (0,) * nd, pipeline_mode=pl.Buffered(1))


def _sigmoid(x):
    return 1.0 / (1.0 + jnp.exp(-x))


def _rms(x, g):
    return x * lax.rsqrt(jnp.mean(x * x, axis=-1, keepdims=True) + EPS) * g


def _mod_kernel(c_ref, w_ref, b_ref, o_ref):
    c = c_ref[...]
    s = c * _sigmoid(c)
    o_ref[...] = jnp.dot(s.astype(BF16), w_ref[...].astype(BF16),
                         preferred_element_type=F32) + b_ref[...]


def _mod_call(c_all, w_ada, b_ada):
    rows = c_all.shape[0]
    n_out = w_ada.shape[1]
    tn = 1024
    return pl.pallas_call(
        _mod_kernel,
        grid=(n_out // tn,),
        in_specs=[pl.BlockSpec((rows, D_MODEL), lambda j: (0, 0)),
                  pl.BlockSpec((D_MODEL, tn), lambda j: (0, j)),
                  pl.BlockSpec((1, tn), lambda j: (0, j))],
        out_specs=pl.BlockSpec((rows, tn), lambda j: (0, j)),
        out_shape=jax.ShapeDtypeStruct((rows, n_out), F32),
        compiler_params=_cparams(("arbitrary",)),
        name="adaln_mod",
    )(c_all, w_ada, b_ada)


def _proj_kernel(x_ref, mod_ref, g_ref, w_ref, cos_ref, sin_ref,
                 q1_ref, k1_ref, v1_ref, q4_ref, k4_ref, v4_ref, q16_ref, k16_ref, v16_ref,
                 u_ref, res_ref, *, tm):
    x = x_ref[...]
    h = _rms(x, g_ref[...]) * (1.0 + mod_ref[1:2, :]) + mod_ref[0:1, :]
    p = jnp.dot(h.astype(BF16), w_ref[...], preferred_element_type=F32)
    cos = jnp.concatenate([cos_ref[...]] * N_PAIRS, axis=1)
    sin = jnp.concatenate([sin_ref[...]] * N_PAIRS, axis=1)
    lane = lax.broadcasted_iota(jnp.int32, (1, ATTN_WIDTH), 1)
    first_half = (lane % HEAD_DIM) < (HEAD_DIM // 2)

    def rope(t):
        fwd = pltpu.roll(t, ATTN_WIDTH - HEAD_DIM // 2, 1)
        bwd = pltpu.roll(t, HEAD_DIM // 2, 1)
        return t * cos + jnp.where(first_half, fwd, bwd) * sin

    qkv = (rope(p[:, 0:ATTN_WIDTH]) * (HEAD_DIM ** -0.5),
           rope(p[:, ATTN_WIDTH:2 * ATTN_WIDTH]),
           p[:, 2 * ATTN_WIDTH:3 * ATTN_WIDTH])
    u_ref[...] = p[:, 3 * ATTN_WIDTH:]
    for a, (t, ref) in enumerate(zip(qkv, (q1_ref, k1_ref, v1_ref))):
        ref[...] = t.astype(BF16)
        for c in range(N_PAIRS):
            res_ref[a * N_PAIRS + c] = t[:, c * PAIR:(c + 1) * PAIR]
    for d, refs in ((4, (q4_ref, k4_ref, v4_ref)), (16, (q16_ref, k16_ref, v16_ref))):
        for r in range(d):
            for a in range(3):
                for c in range(N_PAIRS):
                    piece = res_ref[a * N_PAIRS + c, pl.ds(r, tm // d, stride=d), :]
                    refs[a][r, :, c * PAIR:(c + 1) * PAIR] = piece.astype(BF16)


def _proj_call(x, mod3, norm1_g, w_in, cos_t, sin_t, tm):
    b, s, _ = x.shape
    row = lambda bi, i: (bi, i, 0)
    nat = pl.BlockSpec((None, tm, ATTN_WIDTH), row)
    out_specs, out_shape = [nat] * 3, [jax.ShapeDtypeStruct((b, s, ATTN_WIDTH), BF16)] * 3
    for d in DILATIONS[1:]:
        out_specs += [pl.BlockSpec((None, d, tm // d, ATTN_WIDTH), lambda bi, i: (bi, 0, i, 0))] * 3
        out_shape += [jax.ShapeDtypeStruct((b, d, s // d, ATTN_WIDTH), BF16)] * 3
    out_specs.append(pl.BlockSpec((None, tm, SSM_WIDTH), row))
    out_shape.append(jax.ShapeDtypeStruct((b, s, SSM_WIDTH), F32))
    return pl.pallas_call(
        functools.partial(_proj_kernel, tm=tm),
        grid=(b, s // tm),
        in_specs=[pl.BlockSpec((None, tm, D_MODEL), row),
                  pl.BlockSpec((None, N_MOD, D_MODEL), lambda bi, i: (bi, 0, 0)),
                  _const_spec((1, D_MODEL)),
                  _const_spec((D_MODEL, 4 * ATTN_WIDTH)),
                  pl.BlockSpec((tm, PAIR), lambda bi, i: (i, 0)),
                  pl.BlockSpec((tm, PAIR), lambda bi, i: (i, 0))],
        out_specs=out_specs,
        out_shape=out_shape,
        scratch_shapes=[pltpu.VMEM((3 * N_PAIRS, tm, PAIR), F32)],
        compiler_params=_cparams(("arbitrary", "arbitrary")),
        name="proj_rope",
    )(x, mod3, norm1_g, w_in, cos_t, sin_t)


def _attn_kernel(q_ref, k_ref, v_ref, o_ref, lse_ref, *, sub_len, tq):
    i = pl.program_id(3)
    lane_b = lax.broadcasted_iota(jnp.int32, (Q_TILE, PAIR), 1) < HEAD_DIM
    row = lax.broadcasted_iota(jnp.int32, (2 * Q_TILE, K_WIN), 0) % Q_TILE
    col = lax.broadcasted_iota(jnp.int32, (2 * Q_TILE, K_WIN), 1)
    delta0 = col - row
    for sb in range(tq // Q_TILE):
        m0 = i * tq + sb * Q_TILE
        start = jnp.clip(m0 - HALF_KEYS, 0, sub_len - K_WIN)
        start = pl.multiple_of(start, HALF_KEYS)
        q = q_ref[sb * Q_TILE:(sb + 1) * Q_TILE, :]
        kw = k_ref[pl.ds(start, K_WIN), :]
        vw = v_ref[pl.ds(start, K_WIN), :]
        zero = jnp.zeros_like(q)
        q2 = jnp.concatenate([jnp.where(lane_b, q, zero), jnp.where(lane_b, zero, q)], axis=0)
        s = lax.dot_general(q2, kw, (((1,), (1,)), ((), ())), preferred_element_type=F32)
        valid = jnp.abs(delta0 + (start - m0)) <= HALF_KEYS
        s = jnp.where(valid, s, NEG_BIG)
        m = jnp.max(s, axis=1, keepdims=True)
        p = jnp.exp(s - m)
        l = jnp.sum(p, axis=1, keepdims=True)
        pv = jnp.dot(p.astype(BF16), vw, preferred_element_type=F32)
        o2 = pv / l
        lse2 = jnp.broadcast_to(m + jnp.log(l), (2 * Q_TILE, PAIR))
        o_ref[sb * Q_TILE:(sb + 1) * Q_TILE, :] = jnp.where(lane_b, o2[:Q_TILE], o2[Q_TILE:])
        lse_ref[sb * Q_TILE:(sb + 1) * Q_TILE, :] = jnp.where(lane_b, lse2[:Q_TILE], lse2[Q_TILE:])


def _attn_call(qd, kd, vd):
    b, d, sub_len, _ = qd.shape
    tq = min(1024, sub_len)
    qspec = pl.BlockSpec((None, None, tq, PAIR), lambda bi, r, hp, i: (bi, r, i, hp))
    kspec = pl.BlockSpec((None, None, sub_len, PAIR), lambda bi, r, hp, i: (bi, r, 0, hp))
    out_sd = jax.ShapeDtypeStruct((b, d, sub_len, ATTN_WIDTH), F32)
    return pl.pallas_call(
        functools.partial(_attn_kernel, sub_len=sub_len, tq=tq),
        grid=(b, d, N_PAIRS, sub_len // tq),
        in_specs=[qspec, kspec, kspec],
        out_specs=[qspec, qspec],
        out_shape=[out_sd, out_sd],
        compiler_params=_cparams(("arbitrary",) * 4),
        name=f"dilated_attn_d{d}",
    )(qd, kd, vd)


def _ssm_in_kernel(u_ref, t_ref, we_ref, y_ref, e_ref, lhs_ref, *, mc):
    for j in range(CHUNK):
        lhs_ref[:, j * PAIR:(j + 1) * PAIR] = u_ref[pl.ds(j, mc, stride=CHUNK), :].astype(BF16)
    x = lhs_ref[...]
    y_ref[...] = jnp.dot(x, t_ref[...], preferred_element_type=F32)
    e_ref[...] = jnp.dot(x, we_ref[...], preferred_element_type=F32)


def _ssm_in_call(u, t_w, we_w, mc):
    b, s, _ = u.shape
    n_chunks = s // CHUNK
    tile = pl.BlockSpec((None, None, mc, BUNDLE_W), lambda bun, bi, i: (bi, bun, i, 0))
    wspec = pl.BlockSpec((None, BUNDLE_W, BUNDLE_W), lambda bun, bi, i: (bun, 0, 0),
                         pipeline_mode=pl.Buffered(1))
    out_sd = jax.ShapeDtypeStruct((b, N_BUNDLES, n_chunks, BUNDLE_W), F32)
    return pl.pallas_call(
        functools.partial(_ssm_in_kernel, mc=mc),
        grid=(N_BUNDLES, b, n_chunks // mc),
        in_specs=[pl.BlockSpec((None, CHUNK * mc, PAIR), lambda bun, bi, i: (bi, i, bun)),
                  wspec, wspec],
        out_specs=[tile, tile],
        out_shape=[out_sd, out_sd],
        scratch_shapes=[pltpu.VMEM((mc, BUNDLE_W), BF16)],
        compiler_params=_cparams(("arbitrary",) * 3),
        name="s5_chunk_in",
    )(u, t_w, we_w)


def _ssm_scan_kernel(e_ref, a_ref, h_ref, *, n_chunks):
    rev = pl.program_id(2) == 1
    ar = a_ref[0:1, :]
    ai = a_ref[1:2, :]

    def body(s, carry):
        hr, hi = carry
        k = jnp.where(rev, n_chunks - 1 - s, s)
        h_ref[pl.ds(k, 1), 0:STATE_W] = hr
        h_ref[pl.ds(k, 1), STATE_W:2 * STATE_W] = hi
        er = e_ref[pl.ds(k, 1), 0:STATE_W]
        ei = e_ref[pl.ds(k, 1), STATE_W:2 * STATE_W]
        return ar * hr - ai * hi + er, ar * hi + ai * hr + ei

    zero = jnp.zeros((1, STATE_W), F32)
    lax.fori_loop(0, n_chunks, body, (zero, zero))


def _ssm_scan_call(e, a_pow):
    b, _, n_chunks, _ = e.shape
    tile = pl.BlockSpec((None, None, n_chunks, 2 * STATE_W), lambda bi, bun, dr: (bi, bun, 0, dr))
    return pl.pallas_call(
        functools.partial(_ssm_scan_kernel, n_chunks=n_chunks),
        grid=(b, N_BUNDLES, 2),
        in_specs=[tile,
                  pl.BlockSpec((None, None, 2, STATE_W), lambda bi, bun, dr: (bun, dr, 0, 0))],
        out_specs=tile,
        out_shape=jax.ShapeDtypeStruct(e.shape, F32),
        compiler_params=_cparams(("arbitrary",) * 3),
        name="s5_chunk_scan",
    )(e, a_pow)


def _ssm_out_kernel(h_ref, v_ref, yi_ref, y_ref, *, mc):
    y = yi_ref[...] + jnp.dot(h_ref[...].astype(BF16), v_ref[...], preferred_element_type=F32)
    for t in range(CHUNK):
        y_ref[pl.ds(t, mc, stride=CHUNK), :] = y[:, t * PAIR:(t + 1) * PAIR]


def _ssm_out_call(h_in, v_w, y_intra, mc):
    b, _, n_chunks, _ = h_in.shape
    tile = pl.BlockSpec((None, None, mc, BUNDLE_W), lambda bun, bi, i: (bi, bun, i, 0))
    wspec = pl.BlockSpec((None, BUNDLE_W, BUNDLE_W), lambda bun, bi, i: (bun, 0, 0),
                         pipeline_mode=pl.Buffered(1))
    return pl.pallas_call(
        functools.partial(_ssm_out_kernel, mc=mc),
        grid=(N_BUNDLES, b, n_chunks // mc),
        in_specs=[tile, wspec, tile],
        out_specs=pl.BlockSpec((None, CHUNK * mc, PAIR), lambda bun, bi, i: (bi, i, bun)),
        out_shape=jax.ShapeDtypeStruct((b, n_chunks * CHUNK, SSM_WIDTH), F32),
        compiler_params=_cparams(("arbitrary",) * 3),
        name="s5_chunk_out",
    )(h_in, v_w, y_intra)


def _cpow(lam_re, lam_im, dt, n):
    nn = n.astype(F32).reshape(n.shape + (1, 1))
    mag = jnp.exp(nn * (lam_re * dt))
    ang = nn * (lam_im * dt)
    return mag * jnp.cos(ang), mag * jnp.sin(ang)


def _expand_kernel(c_ref, e_ref, o_ref, *, tr, row_div, col_div):
    w = jnp.dot(c_ref[...].astype(BF16), e_ref[...], preferred_element_type=F32)
    rows = lax.broadcasted_iota(jnp.int32, (tr, BUNDLE_W), 0) + pl.program_id(1) * tr
    cols = lax.broadcasted_iota(jnp.int32, (tr, BUNDLE_W), 1)
    same_group = (rows // row_div) % BUNDLE_GROUPS == (cols // col_div) % BUNDLE_GROUPS
    o_ref[...] = jnp.where(same_group, w, 0.0).astype(BF16)


def _expand_call(compact, spread, row_div, col_div, name):
    tr = 512
    return pl.pallas_call(
        functools.partial(_expand_kernel, tr=tr, row_div=row_div, col_div=col_div),
        grid=(N_BUNDLES, BUNDLE_W // tr),
        in_specs=[pl.BlockSpec((None, tr, 2 * PAIR), lambda bun, i: (bun, i, 0)),
                  pl.BlockSpec((2 * PAIR, BUNDLE_W), lambda bun, i: (0, 0))],
        out_specs=pl.BlockSpec((None, tr, BUNDLE_W), lambda bun, i: (bun, i, 0)),
        out_shape=jax.ShapeDtypeStruct((N_BUNDLES, BUNDLE_W, BUNDLE_W), BF16),
        compiler_params=_cparams(("arbitrary", "arbitrary")),
        name=name,
    )(compact, spread)


def _ssm_weights(lam_re, lam_im, log_dt, b_re, b_im, c_re, c_im):
    hi = lax.Precision.HIGHEST
    steps = jnp.arange(CHUNK)
    k_lag, w_e, v_c, a_pow = [], [], [], []
    for dr in range(2):
        lr, li = lam_re[dr], lam_im[dr]
        dt = jnp.exp(log_dt[dr])[:, None]
        a_r, a_i = _cpow(lr, li, dt, jnp.ones((), F32))
        nr = a_r - 1.0
        den = lr * lr + li * li
        z_r = ((nr * lr + a_i * li) / den)[..., None]
        z_i = ((a_i * lr - nr * li) / den)[..., None]
        bb_r = z_r * b_re[dr] - z_i * b_im[dr]
        bb_i = z_r * b_im[dr] + z_i * b_re[dr]
        cr = jnp.swapaxes(c_re[dr], 1, 2)
        ci = jnp.swapaxes(c_im[dr], 1, 2)

        p_r, p_i = _cpow(lr, li, dt, steps)
        ca_r = p_r[..., None] * cr - p_i[..., None] * ci
        ca_i = p_r[..., None] * ci + p_i[..., None] * cr
        k_lag.append(jnp.einsum('lgpc,gpd->lgdc', ca_r, bb_r, precision=hi)
                     - jnp.einsum('lgpc,gpd->lgdc', ca_i, bb_i, precision=hi))

        e_r, e_i = _cpow(lr, li, dt, (CHUNK - 1 - steps) if dr == 0 else steps)
        we_r = e_r[..., None] * bb_r - e_i[..., None] * bb_i
        we_i = e_r[..., None] * bb_i + e_i[..., None] * bb_r
        w_e.append((we_r, we_i))

        o_r, o_i = _cpow(lr, li, dt, (steps + 1) if dr == 0 else (CHUNK - steps))
        vo_r = o_r[..., None] * cr - o_i[..., None] * ci
        vo_i = o_r[..., None] * ci + o_i[..., None] * cr
        v_c.append((vo_r, -vo_i))

        a_pow.append(_cpow(lr, li, dt, jnp.full((), CHUNK, F32)))

    def bundle(x):
        return x.reshape((x.shape[0], N_BUNDLES, BUNDLE_GROUPS) + x.shape[2:])

    jj = steps[:, None]
    tt = steps[None, :]
    kf = k_lag[0][jnp.clip(tt - jj, 0, CHUNK - 1)] * (tt >= jj)[..., None, None, None].astype(F32)
    kb = k_lag[1][jnp.clip(jj - tt, 0, CHUNK - 1)] * (jj >= tt)[..., None, None, None].astype(F32)
    toe = (kf + kb).reshape(CHUNK, CHUNK, N_BUNDLES, BUNDLE_GROUPS, SSM_GROUP, SSM_GROUP)
    t_c = jnp.transpose(toe, (2, 0, 3, 4, 1, 5)).reshape(N_BUNDLES, BUNDLE_W, 2 * PAIR)

    quarters = [w_e[0][0], w_e[0][1], w_e[1][0], w_e[1][1]]
    we = jnp.stack([bundle(x) for x in quarters], axis=0)
    we_c = jnp.transpose(we, (2, 1, 3, 5, 0, 4)).reshape(N_BUNDLES, BUNDLE_W, 2 * PAIR)

    quarters = [v_c[0][0], v_c[0][1], v_c[1][0], v_c[1][1]]
    vv = jnp.stack([bundle(x) for x in quarters], axis=0)
    v_cmp = jnp.transpose(vv, (2, 0, 3, 4, 1, 5)).reshape(N_BUNDLES, BUNDLE_W, 2 * PAIR)

    src = jnp.arange(2 * PAIR)[:, None]
    dst = jnp.arange(BUNDLE_W)[None, :]
    spread_tc = ((src // SSM_GROUP == dst // PAIR) & (src % SSM_GROUP == dst % SSM_GROUP)).astype(BF16)
    spread_qp = ((src // SSM_STATE == dst // STATE_W) & (src % SSM_STATE == dst % SSM_STATE)).astype(BF16)

    t_w = _expand_call(t_c, spread_tc, SSM_GROUP, SSM_GROUP, "s5_expand_toeplitz")
    we_w = _expand_call(we_c, spread_qp, SSM_GROUP, SSM_STATE, "s5_expand_inject")
    v_w = _expand_call(v_cmp, spread_tc, SSM_STATE, SSM_GROUP, "s5_expand_carry")

    ap = jnp.stack([jnp.stack([a_pow[dr][0], a_pow[dr][1]], axis=0) for dr in range(2)], axis=0)
    ap = ap.reshape(2, 2, N_BUNDLES, STATE_W)
    ap = jnp.transpose(ap, (2, 0, 1, 3))
    return t_w, we_w, v_w, ap


def _post_kernel(x_ref, mod_ref, o1_ref, l1_ref, o4_ref, l4_ref, o16_ref, l16_ref,
                 y_ref, u_ref, dskip_ref, wglu_ref, bglu_ref, ang_ref, sng_ref,
                 wo_ref, n2g_ref, w1_ref, w3_ref, w2_ref, fg_ref, out_ref, nat_ref, *, tm):
    def natural(ref, d, base):
        for r in range(d):
            for c in range(N_PAIRS):
                nat_ref[base + c, pl.ds(r, tm // d, stride=d), :] = ref[r, :, c * PAIR:(c + 1) * PAIR]
        return jnp.concatenate([nat_ref[base + c] for c in range(N_PAIRS)], axis=1)

    o2, l2 = natural(o4_ref, 4, 0), natural(l4_ref, 4, N_PAIRS)
    o3, l3 = natural(o16_ref, 16, 2 * N_PAIRS), natural(l16_ref, 16, 3 * N_PAIRS)
    l1 = l1_ref[...]
    lm = jnp.maximum(jnp.maximum(l1, l2), l3)
    e1, e2, e3 = jnp.exp(l1 - lm), jnp.exp(l2 - lm), jnp.exp(l3 - lm)
    attn = (e1 * o1_ref[...] + e2 * o2 + e3 * o3) / (e1 + e2 + e3)
    an = _rms(attn, ang_ref[...])

    y = y_ref[...] + dskip_ref[...] * u_ref[...]
    g = 0.5 * y * (1.0 + jnp.tanh(math.sqrt(2.0 / math.pi) * (y + 0.044715 * (y * y * y))))
    z = jnp.dot(g.astype(BF16), wglu_ref[...], preferred_element_type=F32) + bglu_ref[...]
    sn = _rms(g * _sigmoid(z), sng_ref[...])

    mixed = (jnp.dot(an.astype(BF16), wo_ref[0:ATTN_WIDTH, :], preferred_element_type=F32)
             + jnp.dot(sn.astype(BF16), wo_ref[ATTN_WIDTH:, :], preferred_element_type=F32))
    x1 = x_ref[...] + mod_ref[2:3, :] * mixed

    h = (_rms(x1, n2g_ref[...]) * (1.0 + mod_ref[4:5, :]) + mod_ref[3:4, :]).astype(BF16)
    a = jnp.dot(h, w1_ref[...], preferred_element_type=F32)
    bgate = jnp.dot(h, w3_ref[...], preferred_element_type=F32)
    hid = (a * _sigmoid(a) * bgate).astype(BF16)
    ffn = jnp.dot(hid, w2_ref[...], preferred_element_type=F32)
    x2 = x1 + mod_ref[5:6, :] * ffn
    out_ref[...] = _rms(x2, fg_ref[...])


def _post_call(x, mod3, branch_outs, y_ssm, u, d_skip, w_glu, b_glu, attn_g, ssm_g,
               w_o, norm2_g, w1, w3, w2, final_g, tm):
    b, s, _ = x.shape
    row = lambda bi, i: (bi, i, 0)
    wide = pl.BlockSpec((None, tm, D_MODEL), row)
    half = pl.BlockSpec((None, tm, ATTN_WIDTH), row)
    res = lambda d: pl.BlockSpec((None, d, tm // d, ATTN_WIDTH), lambda bi, i: (bi, 0, i, 0))
    return pl.pallas_call(
        functools.partial(_post_kernel, tm=tm),
        grid=(b, s // tm),
        in_specs=[wide, pl.BlockSpec((None, N_MOD, D_MODEL), lambda bi, i: (bi, 0, 0)),
                  half, half, res(4), res(4), res(16), res(16), half, half,
                  _const_spec((1, SSM_WIDTH)), _const_spec((SSM_WIDTH, SSM_WIDTH)),
                  _const_spec((1, SSM_WIDTH)), _const_spec((1, ATTN_WIDTH)),
                  _const_spec((1, SSM_WIDTH)), _const_spec((D_MODEL, D_MODEL)),
                  _const_spec((1, D_MODEL)), _const_spec((D_MODEL, FFN_HIDDEN)),
                  _const_spec((D_MODEL, FFN_HIDDEN)), _const_spec((FFN_HIDDEN, D_MODEL)),
                  _const_spec((1, D_MODEL))],
        out_specs=wide,
        out_shape=jax.ShapeDtypeStruct((b, s, D_MODEL), F32),
        scratch_shapes=[pltpu.VMEM((4 * N_PAIRS, tm, PAIR), F32)],
        compiler_params=_cparams(("arbitrary", "arbitrary")),
        name="merge_mix_ffn",
    )(x, mod3, *branch_outs, y_ssm, u, d_skip, w_glu, b_glu, attn_g, ssm_g,
      w_o, norm2_g, w1, w3, w2, final_g)


def _rope_tables(seq_len):
    inv = 1.0 / (ROPE_THETA ** (jnp.arange(0, HEAD_DIM, 2, dtype=F32) / HEAD_DIM))
    ang = jnp.arange(seq_len, dtype=F32)[:, None] * inv[None, :]
    cos, sin = jnp.cos(ang), jnp.sin(ang)
    return (jnp.concatenate([cos, cos] * 2, axis=1),
            jnp.concatenate([-sin, sin] * 2, axis=1))


def _encode(x, mod3, p):
    b, s, _ = x.shape
    n_chunks = s // CHUNK
    cos_t, sin_t = _rope_tables(s)
    (q1, k1, v1, q4, k4, v4, q16, k16, v16, u) = _proj_call(
        x, mod3, p["norm1_g"], p["w_in"], cos_t, sin_t, tm=512)

    o1, l1 = _attn_call(q1[:, None], k1[:, None], v1[:, None])
    o4, l4 = _attn_call(q4, k4, v4)
    o16, l16 = _attn_call(q16, k16, v16)
    branch_outs = (o1[:, 0], l1[:, 0], o4, l4, o16, l16)

    mc = min(256, n_chunks)
    y_intra, e = _ssm_in_call(u, p["t_w"], p["we_w"], mc)
    h_in = _ssm_scan_call(e, p["a_pow"])
    y_ssm = _ssm_out_call(h_in, p["v_w"], y_intra, mc)

    return _post_call(x, mod3, branch_outs, y_ssm, u, p["d_skip"], p["w_glu"], p["b_glu"],
                      p["attn_norm_g"], p["ssm_norm_g"], p["w_o"], p["norm2_g"],
                      p["w1"], p["w3"], p["w2"], p["final_g"], tm=256)


def kernel(x_prompt, x_sample, c_prompt, c_sample, w_ada, b_ada, norm1_g, w_in, lam_re, lam_im,
           log_dt, b_re, b_im, c_re, c_im, d_skip, w_glu, b_glu, attn_norm_g, ssm_norm_g, w_o,
           norm2_g, w1, w3, w2, final_g):
    nb_p, nb_s = c_prompt.shape[0], c_sample.shape[0]
    rows = -(-(nb_p + nb_s) // 8) * 8
    c_all = jnp.concatenate([c_prompt, c_sample,
                             jnp.zeros((rows - nb_p - nb_s, D_MODEL), F32)], axis=0)
    mod = _mod_call(c_all, w_ada[0], b_ada[0][None, :]).reshape(rows, N_MOD, D_MODEL)

    t_w, we_w, v_w, a_pow = _ssm_weights(lam_re[0], lam_im[0], log_dt[0], b_re[0], b_im[0],
                                         c_re[0], c_im[0])
    p = dict(norm1_g=norm1_g[0][None, :], w_in=w_in[0].astype(BF16),
             t_w=t_w, we_w=we_w, v_w=v_w, a_pow=a_pow,
             d_skip=d_skip[0][None, :], w_glu=w_glu[0].astype(BF16), b_glu=b_glu[0][None, :],
             attn_norm_g=attn_norm_g[0][None, :], ssm_norm_g=ssm_norm_g[0][None, :],
             w_o=w_o[0].astype(BF16), norm2_g=norm2_g[0][None, :],
             w1=w1[0].astype(BF16), w3=w3[0].astype(BF16), w2=w2[0].astype(BF16),
             final_g=final_g[None, :])
    y_prompt = _encode(x_prompt, mod[:nb_p], p)
    y_sample = _encode(x_sample, mod[nb_p:nb_p + nb_s], p)
    return (y_prompt, y_sample)
```

```python
import functools
import math

import jax
import jax.numpy as jnp
from jax import lax
from jax.experimental import pallas as pl
from jax.experimental.pallas import tpu as pltpu

F32 = jnp.float32
BF16 = jnp.bfloat16

D_MODEL = 1024
ATTN_WIDTH = 512
SSM_WIDTH = 512
HEAD_DIM = 64
PAIR = 2 * HEAD_DIM
N_PAIRS = ATTN_WIDTH // PAIR
DILATIONS = (1, 4, 16)
HALF_KEYS = 64
Q_TILE = 128
K_WIN = Q_TILE + 2 * HALF_KEYS
ROPE_THETA = 10000.0
SSM_GROUP = 16
N_GROUPS = SSM_WIDTH // SSM_GROUP
SSM_STATE = 64
CHUNK = 16
BUNDLE_GROUPS = 8
N_BUNDLES = N_GROUPS // BUNDLE_GROUPS
BUNDLE_W = CHUNK * BUNDLE_GROUPS * SSM_GROUP
STATE_W = BUNDLE_GROUPS * SSM_STATE
FFN_HIDDEN = 2816
N_MOD = 6
EPS = 1e-6
NEG_BIG = -1e30
VMEM_LIMIT = 56 * 1024 * 1024


def _cparams(sem):
    return pltpu.CompilerParams(dimension_semantics=sem, vmem_limit_bytes=VMEM_LIMIT)


def _const_spec(shape):
    nd = len(shape)
    return pl.BlockSpec(shape, lambda *_: (0,) * nd, pipeline_mode=pl.Buffered(1))


def _sigmoid(x):
    return 1.0 / (1.0 + jnp.exp(-x))


def _rms(x, g):
    return x * lax.rsqrt(jnp.mean(x * x, axis=-1, keepdims=True) + EPS) * g


def _mod_kernel(c_ref, w_ref, b_ref, o_ref):
    c = c_ref[...]
    s = c * _sigmoid(c)
    o_ref[...] = jnp.dot(s.astype(BF16), w_ref[...].astype(BF16),
                         preferred_element_type=F32) + b_ref[...]


def _mod_call(c_all, w_ada, b_ada):
    rows = c_all.shape[0]
    n_out = w_ada.shape[1]
    tn = 1024
    return pl.pallas_call(
        _mod_kernel,
        grid=(n_out // tn,),
        in_specs=[pl.BlockSpec((rows, D_MODEL), lambda j: (0, 0)),
                  pl.BlockSpec((D_MODEL, tn), lambda j: (0, j)),
                  pl.BlockSpec((1, tn), lambda j: (0, j))],
        out_specs=pl.BlockSpec((rows, tn), lambda j: (0, j)),
        out_shape=jax.ShapeDtypeStruct((rows, n_out), F32),
        compiler_params=_cparams(("arbitrary",)),
        name="adaln_mod",
    )(c_all, w_ada, b_ada)


def _proj_kernel(x_ref, mod_ref, g_ref, w_ref, cos_ref, sin_ref,
                 q1_ref, k1_ref, v1_ref, q4_ref, k4_ref, v4_ref, q16_ref, k16_ref, v16_ref,
                 u_ref, res_ref, tmp_ref, *, tm):
    x = x_ref[...]
    h = _rms(x, g_ref[...]) * (1.0 + mod_ref[1:2, :]) + mod_ref[0:1, :]
    p = jnp.dot(h.astype(BF16), w_ref[...], preferred_element_type=F32)
    cos = jnp.concatenate([cos_ref[...]] * N_PAIRS, axis=1)
    sin = jnp.concatenate([sin_ref[...]] * N_PAIRS, axis=1)
    lane = lax.broadcasted_iota(jnp.int32, (1, ATTN_WIDTH), 1)
    first_half = (lane % HEAD_DIM) < (HEAD_DIM // 2)

    def rope(t):
        fwd = pltpu.roll(t, ATTN_WIDTH - HEAD_DIM // 2, 1)
        bwd = pltpu.roll(t, HEAD_DIM // 2, 1)
        return t * cos + jnp.where(first_half, fwd, bwd) * sin

    qkv = (rope(p[:, 0:ATTN_WIDTH]) * (HEAD_DIM ** -0.5),
           rope(p[:, ATTN_WIDTH:2 * ATTN_WIDTH]),
           p[:, 2 * ATTN_WIDTH:3 * ATTN_WIDTH])
    u_ref[...] = p[:, 3 * ATTN_WIDTH:]
    for a, (t, ref) in enumerate(zip(qkv, (q1_ref, k1_ref, v1_ref))):
        ref[...] = t.astype(BF16)
        for c in range(N_PAIRS):
            res_ref[a * N_PAIRS + c] = t[:, c * PAIR:(c + 1) * PAIR]
    n4, n16 = tm // 4, tm // 16
    for a, (ref4, ref16) in enumerate(((q4_ref, q16_ref), (k4_ref, k16_ref), (v4_ref, v16_ref))):
        for c in range(N_PAIRS):
            slab = a * N_PAIRS + c
            cols = slice(c * PAIR, (c + 1) * PAIR)
            for r4 in range(4):
                t4 = res_ref[slab, pl.ds(r4, n4, stride=4), :]
                ref4[r4, :, cols] = t4.astype(BF16)
                tmp_ref[slab, r4 * n4:(r4 + 1) * n4, :] = t4
            for r4 in range(4):
                for rp in range(4):
                    piece = tmp_ref[slab, pl.ds(r4 * n4 + rp, n16, stride=4), :]
                    ref16[r4 + 4 * rp, :, cols] = piece.astype(BF16)


def _proj_call(x, mod3, norm1_g, w_in, cos_t, sin_t, tm):
    b, s, _ = x.shape
    row = lambda bi, i: (bi, i, 0)
    nat = pl.BlockSpec((None, tm, ATTN_WIDTH), row)
    out_specs, out_shape = [nat] * 3, [jax.ShapeDtypeStruct((b, s, ATTN_WIDTH), BF16)] * 3
    for d in DILATIONS[1:]:
        out_specs += [pl.BlockSpec((None, d, tm // d, ATTN_WIDTH), lambda bi, i: (bi, 0, i, 0))] * 3
        out_shape += [jax.ShapeDtypeStruct((b, d, s // d, ATTN_WIDTH), BF16)] * 3
    out_specs.append(pl.BlockSpec((None, tm, SSM_WIDTH), row))
    out_shape.append(jax.ShapeDtypeStruct((b, s, SSM_WIDTH), F32))
    return pl.pallas_call(
        functools.partial(_proj_kernel, tm=tm),
        grid=(b, s // tm),
        in_specs=[pl.BlockSpec((None, tm, D_MODEL), row),
                  pl.BlockSpec((None, N_MOD, D_MODEL), lambda bi, i: (bi, 0, 0)),
                  _const_spec((1, D_MODEL)),
                  _const_spec((D_MODEL, 4 * ATTN_WIDTH)),
                  pl.BlockSpec((tm, PAIR), lambda bi, i: (i, 0)),
                  pl.BlockSpec((tm, PAIR), lambda bi, i: (i, 0))],
        out_specs=out_specs,
        out_shape=out_shape,
        scratch_shapes=[pltpu.VMEM((3 * N_PAIRS, tm, PAIR), F32)] * 2,
        compiler_params=_cparams(("arbitrary", "arbitrary")),
        name="proj_rope",
    )(x, mod3, norm1_g, w_in, cos_t, sin_t)


def _attn_kernel(bias_ref, q_ref, k_ref, v_ref, o_ref, lse_ref, *, sub_len, tq, rr, pp):
    i = pl.program_id(3)
    lane_b = lax.broadcasted_iota(jnp.int32, (Q_TILE, PAIR), 1) < HEAD_DIM

    def one_residue(r):
        for sb in range(tq // Q_TILE):
            rows = slice(sb * Q_TILE, (sb + 1) * Q_TILE)
            m0 = i * tq + sb * Q_TILE
            start = jnp.clip(m0 - HALF_KEYS, 0, sub_len - K_WIN)
            start = pl.multiple_of(start, HALF_KEYS)
            bias = bias_ref[jnp.where(m0 == 0, 1, jnp.where(m0 == sub_len - Q_TILE, 2, 0))]
            for hp in range(pp):
                cols = slice(hp * PAIR, (hp + 1) * PAIR)
                q = q_ref[r, rows, cols]
                kw = k_ref[r, pl.ds(start, K_WIN), cols]
                vw = v_ref[r, pl.ds(start, K_WIN), cols]
                zero = jnp.zeros_like(q)
                q2 = jnp.concatenate([jnp.where(lane_b, q, zero), jnp.where(lane_b, zero, q)],
                                     axis=0)
                s = lax.dot_general(q2, kw, (((1,), (1,)), ((), ())),
                                    preferred_element_type=F32) + bias
                m = jnp.max(s, axis=1, keepdims=True)
                p = jnp.exp(s - m)
                l = jnp.sum(p, axis=1, keepdims=True)
                pv = jnp.dot(p.astype(BF16), vw, preferred_element_type=F32)
                o2 = pv / l
                lse2 = jnp.broadcast_to(m + jnp.log(l), (2 * Q_TILE, PAIR))
                o_ref[r, rows, cols] = jnp.where(lane_b, o2[:Q_TILE], o2[Q_TILE:])
                lse_ref[r, rows, cols] = jnp.where(lane_b, lse2[:Q_TILE], lse2[Q_TILE:])

    if rr == 1:
        one_residue(0)
    else:
        def body(r, carry):
            one_residue(r)
            return carry
        lax.fori_loop(0, rr, body, 0)


def _band_bias():
    row = lax.broadcasted_iota(jnp.int32, (2 * Q_TILE, K_WIN), 0) % Q_TILE
    col = lax.broadcasted_iota(jnp.int32, (2 * Q_TILE, K_WIN), 1)
    offs = jnp.array([-HALF_KEYS, 0, -2 * HALF_KEYS], jnp.int32)[:, None, None]
    valid = jnp.abs(col - row + offs) <= HALF_KEYS
    return jnp.where(valid, 0.0, NEG_BIG).astype(F32)


def _attn_call(qd, kd, vd, bias):
    b, d, sub_len, _ = qd.shape
    pp = max(1, min(N_PAIRS, (4 * 1024 * 1024) // (sub_len * PAIR * 2)))
    tq = min(512 if pp > 1 else 1024, sub_len)
    rr = max(1, min(d, 16 // (pp * (tq // Q_TILE))))
    qspec = pl.BlockSpec((None, rr, tq, pp * PAIR), lambda bi, r, hp, i: (bi, r, i, hp))
    kspec = pl.BlockSpec((None, rr, sub_len, pp * PAIR), lambda bi, r, hp, i: (bi, r, 0, hp))
    out_sd = jax.ShapeDtypeStruct((b, d, sub_len, ATTN_WIDTH), F32)
    return pl.pallas_call(
        functools.partial(_attn_kernel, sub_len=sub_len, tq=tq, rr=rr, pp=pp),
        grid=(b, d // rr, N_PAIRS // pp, sub_len // tq),
        in_specs=[_const_spec((3, 2 * Q_TILE, K_WIN)), qspec, kspec, kspec],
        out_specs=[qspec, qspec],
        out_shape=[out_sd, out_sd],
        compiler_params=_cparams(("arbitrary",) * 4),
        name=f"dilated_attn_d{d}",
    )(bias, qd, kd, vd)


def _ssm_in_kernel(u_ref, t_ref, we_ref, y_ref, e_ref, lhs_ref, *, mc):
    for j in range(CHUNK):
        lhs_ref[:, j * PAIR:(j + 1) * PAIR] = u_ref[pl.ds(j, mc, stride=CHUNK), :].astype(BF16)
    x = lhs_ref[...]
    y_ref[...] = jnp.dot(x, t_ref[...], preferred_element_type=F32)
    e_ref[...] = jnp.dot(x, we_ref[...], preferred_element_type=F32)


def _ssm_in_call(u, t_w, we_w, mc):
    b, s, _ = u.shape
    n_chunks = s // CHUNK
    tile = pl.BlockSpec((None, None, mc, BUNDLE_W), lambda bun, bi, i: (bi, bun, i, 0))
    wspec = pl.BlockSpec((None, BUNDLE_W, BUNDLE_W), lambda bun, bi, i: (bun, 0, 0),
                         pipeline_mode=pl.Buffered(1))
    out_sd = jax.ShapeDtypeStruct((b, N_BUNDLES, n_chunks, BUNDLE_W), F32)
    return pl.pallas_call(
        functools.partial(_ssm_in_kernel, mc=mc),
        grid=(N_BUNDLES, b, n_chunks // mc),
        in_specs=[pl.BlockSpec((None, CHUNK * mc, PAIR), lambda bun, bi, i: (bi, i, bun)),
                  wspec, wspec],
        out_specs=[tile, tile],
        out_shape=[out_sd, out_sd],
        scratch_shapes=[pltpu.VMEM((mc, BUNDLE_W), BF16)],
        compiler_params=_cparams(("arbitrary",) * 3),
        name="s5_chunk_in",
    )(u, t_w, we_w)


def _ssm_scan_kernel(e_ref, a_ref, h_ref, *, n_chunks):
    rev = pl.program_id(2) == 1
    ar = a_ref[0:1, :]
    ai = a_ref[1:2, :]

    def body(s, carry):
        hr, hi = carry
        k = jnp.where(rev, n_chunks - 1 - s, s)
        h_ref[pl.ds(k, 1), 0:STATE_W] = hr
        h_ref[pl.ds(k, 1), STATE_W:2 * STATE_W] = hi
        er = e_ref[pl.ds(k, 1), 0:STATE_W]
        ei = e_ref[pl.ds(k, 1), STATE_W:2 * STATE_W]
        return ar * hr - ai * hi + er, ar * hi + ai * hr + ei

    zero = jnp.zeros((1, STATE_W), F32)
    lax.fori_loop(0, n_chunks, body, (zero, zero))


def _ssm_scan_call(e, a_pow):
    b, _, n_chunks, _ = e.shape
    tile = pl.BlockSpec((None, None, n_chunks, 2 * STATE_W), lambda bi, bun, dr: (bi, bun, 0, dr))
    return pl.pallas_call(
        functools.partial(_ssm_scan_kernel, n_chunks=n_chunks),
        grid=(b, N_BUNDLES, 2),
        in_specs=[tile,
                  pl.BlockSpec((None, None, 2, STATE_W), lambda bi, bun, dr: (bun, dr, 0, 0))],
        out_specs=tile,
        out_shape=jax.ShapeDtypeStruct(e.shape, F32),
        compiler_params=_cparams(("arbitrary",) * 3),
        name="s5_chunk_scan",
    )(e, a_pow)


def _ssm_out_kernel(h_ref, v_ref, yi_ref, y_ref, *, mc):
    y = yi_ref[...] + jnp.dot(h_ref[...].astype(BF16), v_ref[...], preferred_element_type=F32)
    for t in range(CHUNK):
        y_ref[pl.ds(t, mc, stride=CHUNK), :] = y[:, t * PAIR:(t + 1) * PAIR]


def _ssm_out_call(h_in, v_w, y_intra, mc):
    b, _, n_chunks, _ = h_in.shape
    tile = pl.BlockSpec((None, None, mc, BUNDLE_W), lambda bun, bi, i: (bi, bun, i, 0))
    wspec = pl.BlockSpec((None, BUNDLE_W, BUNDLE_W), lambda bun, bi, i: (bun, 0, 0),
                         pipeline_mode=pl.Buffered(1))
    return pl.pallas_call(
        functools.partial(_ssm_out_kernel, mc=mc),
        grid=(N_BUNDLES, b, n_chunks // mc),
        in_specs=[tile, wspec, tile],
        out_specs=pl.BlockSpec((None, CHUNK * mc, PAIR), lambda bun, bi, i: (bi, i, bun)),
        out_shape=jax.ShapeDtypeStruct((b, n_chunks * CHUNK, SSM_WIDTH), F32),
        compiler_params=_cparams(("arbitrary",) * 3),
        name="s5_chunk_out",
    )(h_in, v_w, y_intra)


def _cpow(lam_re, lam_im, dt, n):
    nn = n.astype(F32).reshape(n.shape + (1, 1))
    mag = jnp.exp(nn * (lam_re * dt))
    ang = nn * (lam_im * dt)
    return mag * jnp.cos(ang), mag * jnp.sin(ang)


def _expand_kernel(c_ref, e_ref, o_ref, *, tr, row_div, col_div):
    w = jnp.dot(c_ref[...].astype(BF16), e_ref[...], preferred_element_type=F32)
    rows = lax.broadcasted_iota(jnp.int32, (tr, BUNDLE_W), 0) + pl.program_id(1) * tr
    cols = lax.broadcasted_iota(jnp.int32, (tr, BUNDLE_W), 1)
    same_group = (rows // row_div) % BUNDLE_GROUPS == (cols // col_div) % BUNDLE_GROUPS
    o_ref[...] = jnp.where(same_group, w, 0.0).astype(BF16)


def _expand_call(compact, spread, row_div, col_div, name):
    tr = 512
    return pl.pallas_call(
        functools.partial(_expand_kernel, tr=tr, row_div=row_div, col_div=col_div),
        grid=(N_BUNDLES, BUNDLE_W // tr),
        in_specs=[pl.BlockSpec((None, tr, 2 * PAIR), lambda bun, i: (bun, i, 0)),
                  pl.BlockSpec((2 * PAIR, BUNDLE_W), lambda bun, i: (0, 0))],
        out_specs=pl.BlockSpec((None, tr, BUNDLE_W), lambda bun, i: (bun, i, 0)),
        out_shape=jax.ShapeDtypeStruct((N_BUNDLES, BUNDLE_W, BUNDLE_W), BF16),
        compiler_params=_cparams(("arbitrary", "arbitrary")),
        name=name,
    )(compact, spread)


def _ssm_weights(lam_re, lam_im, log_dt, b_re, b_im, c_re, c_im):
    hi = lax.Precision.HIGHEST
    steps = jnp.arange(CHUNK)
    k_lag, w_e, v_c, a_pow = [], [], [], []
    for dr in range(2):
        lr, li = lam_re[dr], lam_im[dr]
        dt = jnp.exp(log_dt[dr])[:, None]
        a_r, a_i = _cpow(lr, li, dt, jnp.ones((), F32))
        nr = a_r - 1.0
        den = lr * lr + li * li
        z_r = ((nr * lr + a_i * li) / den)[..., None]
        z_i = ((a_i * lr - nr * li) / den)[..., None]
        bb_r = z_r * b_re[dr] - z_i * b_im[dr]
        bb_i = z_r * b_im[dr] + z_i * b_re[dr]
        cr = jnp.swapaxes(c_re[dr], 1, 2)
        ci = jnp.swapaxes(c_im[dr], 1, 2)

        p_r, p_i = _cpow(lr, li, dt, steps)
        ca_r = p_r[..., None] * cr - p_i[..., None] * ci
        ca_i = p_r[..., None] * ci + p_i[..., None] * cr
        k_lag.append(jnp.einsum('lgpc,gpd->lgdc', ca_r, bb_r, precision=hi)
                     - jnp.einsum('lgpc,gpd->lgdc', ca_i, bb_i, precision=hi))

        e_r, e_i = _cpow(lr, li, dt, (CHUNK - 1 - steps) if dr == 0 else steps)
        we_r = e_r[..., None] * bb_r - e_i[..., None] * bb_i
        we_i = e_r[..., None] * bb_i + e_i[..., None] * bb_r
        w_e.append((we_r, we_i))

        o_r, o_i = _cpow(lr, li, dt, (steps + 1) if dr == 0 else (CHUNK - steps))
        vo_r = o_r[..., None] * cr - o_i[..., None] * ci
        vo_i = o_r[..., None] * ci + o_i[..., None] * cr
        v_c.append((vo_r, -vo_i))

        a_pow.append(_cpow(lr, li, dt, jnp.full((), CHUNK, F32)))

    def bundle(x):
        return x.reshape((x.shape[0], N_BUNDLES, BUNDLE_GROUPS) + x.shape[2:])

    jj = steps[:, None]
    tt = steps[None, :]
    kf = k_lag[0][jnp.clip(tt - jj, 0, CHUNK - 1)] * (tt >= jj)[..., None, None, None].astype(F32)
    kb = k_lag[1][jnp.clip(jj - tt, 0, CHUNK - 1)] * (jj >= tt)[..., None, None, None].astype(F32)
    toe = (kf + kb).reshape(CHUNK, CHUNK, N_BUNDLES, BUNDLE_GROUPS, SSM_GROUP, SSM_GROUP)
    t_c = jnp.transpose(toe, (2, 0, 3, 4, 1, 5)).reshape(N_BUNDLES, BUNDLE_W, 2 * PAIR)

    quarters = [w_e[0][0], w_e[0][1], w_e[1][0], w_e[1][1]]
    we = jnp.stack([bundle(x) for x in quarters], axis=0)
    we_c = jnp.transpose(we, (2, 1, 3, 5, 0, 4)).reshape(N_BUNDLES, BUNDLE_W, 2 * PAIR)

    quarters = [v_c[0][0], v_c[0][1], v_c[1][0], v_c[1][1]]
    vv = jnp.stack([bundle(x) for x in quarters], axis=0)
    v_cmp = jnp.transpose(vv, (2, 0, 3, 4, 1, 5)).reshape(N_BUNDLES, BUNDLE_W, 2 * PAIR)

    src = jnp.arange(2 * PAIR)[:, None]
    dst = jnp.arange(BUNDLE_W)[None, :]
    spread_tc = ((src // SSM_GROUP == dst // PAIR) & (src % SSM_GROUP == dst % SSM_GROUP)).astype(BF16)
    spread_qp = ((src // SSM_STATE == dst // STATE_W) & (src % SSM_STATE == dst % SSM_STATE)).astype(BF16)

    t_w = _expand_call(t_c, spread_tc, SSM_GROUP, SSM_GROUP, "s5_expand_toeplitz")
    we_w = _expand_call(we_c, spread_qp, SSM_GROUP, SSM_STATE, "s5_expand_inject")
    v_w = _expand_call(v_cmp, spread_tc, SSM_STATE, SSM_GROUP, "s5_expand_carry")

    ap = jnp.stack([jnp.stack([a_pow[dr][0], a_pow[dr][1]], axis=0) for dr in range(2)], axis=0)
    ap = ap.reshape(2, 2, N_BUNDLES, STATE_W)
    ap = jnp.transpose(ap, (2, 0, 1, 3))
    return t_w, we_w, v_w, ap


def _post_kernel(x_ref, mod_ref, o1_ref, l1_ref, o4_ref, l4_ref, o16_ref, l16_ref,
                 y_ref, u_ref, dskip_ref, wglu_ref, bglu_ref, ang_ref, sng_ref,
                 wo_ref, n2g_ref, w1_ref, w3_ref, w2_ref, fg_ref, out_ref, nat_ref, tmp_ref, *, tm):
    n4, n16 = tm // 4, tm // 16

    def natural(ref, d, base):
        for c in range(N_PAIRS):
            cols = slice(c * PAIR, (c + 1) * PAIR)
            for r4 in range(4):
                if d == 16:
                    for rp in range(4):
                        tmp_ref[c, pl.ds(r4 * n4 + rp, n16, stride=4), :] = ref[r4 + 4 * rp, :, cols]
                    quarter = tmp_ref[c, r4 * n4:(r4 + 1) * n4, :]
                else:
                    quarter = ref[r4, :, cols]
                nat_ref[base + c, pl.ds(r4, n4, stride=4), :] = quarter
        return jnp.concatenate([nat_ref[base + c] for c in range(N_PAIRS)], axis=1)

    o2, l2 = natural(o4_ref, 4, 0), natural(l4_ref, 4, N_PAIRS)
    o3, l3 = natural(o16_ref, 16, 2 * N_PAIRS), natural(l16_ref, 16, 3 * N_PAIRS)
    l1 = l1_ref[...]
    lm = jnp.maximum(jnp.maximum(l1, l2), l3)
    e1, e2, e3 = jnp.exp(l1 - lm), jnp.exp(l2 - lm), jnp.exp(l3 - lm)
    attn = (e1 * o1_ref[...] + e2 * o2 + e3 * o3) / (e1 + e2 + e3)
    an = _rms(attn, ang_ref[...])

    y = y_ref[...] + dskip_ref[...] * u_ref[...]
    g = 0.5 * y * (1.0 + jnp.tanh(math.sqrt(2.0 / math.pi) * (y + 0.044715 * (y * y * y))))
    z = jnp.dot(g.astype(BF16), wglu_ref[...], preferred_element_type=F32) + bglu_ref[...]
    sn = _rms(g * _sigmoid(z), sng_ref[...])

    mixed = (jnp.dot(an.astype(BF16), wo_ref[0:ATTN_WIDTH, :], preferred_element_type=F32)
             + jnp.dot(sn.astype(BF16), wo_ref[ATTN_WIDTH:, :], preferred_element_type=F32))
    x1 = x_ref[...] + mod_ref[2:3, :] * mixed

    h = (_rms(x1, n2g_ref[...]) * (1.0 + mod_ref[4:5, :]) + mod_ref[3:4, :]).astype(BF16)
    a = jnp.dot(h, w1_ref[...], preferred_element_type=F32)
    bgate = jnp.dot(h, w3_ref[...], preferred_element_type=F32)
    hid = (a * _sigmoid(a) * bgate).astype(BF16)
    ffn = jnp.dot(hid, w2_ref[...], preferred_element_type=F32)
    x2 = x1 + mod_ref[5:6, :] * ffn
    out_ref[...] = _rms(x2, fg_ref[...])


def _post_call(x, mod3, branch_outs, y_ssm, u, d_skip, w_glu, b_glu, attn_g, ssm_g,
               w_o, norm2_g, w1, w3, w2, final_g, tm):
    b, s, _ = x.shape
    row = lambda bi, i: (bi, i, 0)
    wide = pl.BlockSpec((None, tm, D_MODEL), row)
    half = pl.BlockSpec((None, tm, ATTN_WIDTH), row)
    res = lambda d: pl.BlockSpec((None, d, tm // d, ATTN_WIDTH), lambda bi, i: (bi, 0, i, 0))
    return pl.pallas_call(
        functools.partial(_post_kernel, tm=tm),
        grid=(b, s // tm),
        in_specs=[wide, pl.BlockSpec((None, N_MOD, D_MODEL), lambda bi, i: (bi, 0, 0)),
                  half, half, res(4), res(4), res(16), res(16), half, half,
                  _const_spec((1, SSM_WIDTH)), _const_spec((SSM_WIDTH, SSM_WIDTH)),
                  _const_spec((1, SSM_WIDTH)), _const_spec((1, ATTN_WIDTH)),
                  _const_spec((1, SSM_WIDTH)), _const_spec((D_MODEL, D_MODEL)),
                  _const_spec((1, D_MODEL)), _const_spec((D_MODEL, FFN_HIDDEN)),
                  _const_spec((D_MODEL, FFN_HIDDEN)), _const_spec((FFN_HIDDEN, D_MODEL)),
                  _const_spec((1, D_MODEL))],
        out_specs=wide,
        out_shape=jax.ShapeDtypeStruct((b, s, D_MODEL), F32),
        scratch_shapes=[pltpu.VMEM((4 * N_PAIRS, tm, PAIR), F32),
                        pltpu.VMEM((N_PAIRS, tm, PAIR), F32)],
        compiler_params=_cparams(("arbitrary", "arbitrary")),
        name="merge_mix_ffn",
    )(x, mod3, *branch_outs, y_ssm, u, d_skip, w_glu, b_glu, attn_g, ssm_g,
      w_o, norm2_g, w1, w3, w2, final_g)


def _rope_tables(seq_len):
    inv = 1.0 / (ROPE_THETA ** (jnp.arange(0, HEAD_DIM, 2, dtype=F32) / HEAD_DIM))
    ang = jnp.arange(seq_len, dtype=F32)[:, None] * inv[None, :]
    cos, sin = jnp.cos(ang), jnp.sin(ang)
    return (jnp.concatenate([cos, cos] * 2, axis=1),
            jnp.concatenate([-sin, sin] * 2, axis=1))


def _encode(x, mod3, p):
    b, s, _ = x.shape
    n_chunks = s // CHUNK
    cos_t, sin_t = _rope_tables(s)
    (q1, k1, v1, q4, k4, v4, q16, k16, v16, u) = _proj_call(
        x, mod3, p["norm1_g"], p["w_in"], cos_t, sin_t, tm=512)

    bias = _band_bias()
    o1, l1 = _attn_call(q1[:, None], k1[:, None], v1[:, None], bias)
    o4, l4 = _attn_call(q4, k4, v4, bias)
    o16, l16 = _attn_call(q16, k16, v16, bias)
    branch_outs = (o1[:, 0], l1[:, 0], o4, l4, o16, l16)

    mc = min(256, n_chunks)
    y_intra, e = _ssm_in_call(u, p["t_w"], p["we_w"], mc)
    h_in = _ssm_scan_call(e, p["a_pow"])
    y_ssm = _ssm_out_call(h_in, p["v_w"], y_intra, mc)

    return _post_call(x, mod3, branch_outs, y_ssm, u, p["d_skip"], p["w_glu"], p["b_glu"],
                      p["attn_norm_g"], p["ssm_norm_g"], p["w_o"], p["norm2_g"],
                      p["w1"], p["w3"], p["w2"], p["final_g"], tm=256)


def kernel(x_prompt, x_sample, c_prompt, c_sample, w_ada, b_ada, norm1_g, w_in, lam_re, lam_im,
           log_dt, b_re, b_im, c_re, c_im, d_skip, w_glu, b_glu, attn_norm_g, ssm_norm_g, w_o,
           norm2_g, w1, w3, w2, final_g):
    nb_p, nb_s = c_prompt.shape[0], c_sample.shape[0]
    rows = -(-(nb_p + nb_s) // 8) * 8
    c_all = jnp.concatenate([c_prompt, c_sample,
                             jnp.zeros((rows - nb_p - nb_s, D_MODEL), F32)], axis=0)
    mod = _mod_call(c_all, w_ada[0], b_ada[0][None, :]).reshape(rows, N_MOD, D_MODEL)

    t_w, we_w, v_w, a_pow = _ssm_weights(lam_re[0], lam_im[0], log_dt[0], b_re[0], b_im[0],
                                         c_re[0], c_im[0])
    p = dict(norm1_g=norm1_g[0][None, :], w_in=w_in[0].astype(BF16),
             t_w=t_w, we_w=we_w, v_w=v_w, a_pow=a_pow,
             d_skip=d_skip[0][None, :], w_glu=w_glu[0].astype(BF16), b_glu=b_glu[0][None, :],
             attn_norm_g=attn_norm_g[0][None, :], ssm_norm_g=ssm_norm_g[0][None, :],
             w_o=w_o[0].astype(BF16), norm2_g=norm2_g[0][None, :],
             w1=w1[0].astype(BF16), w3=w3[0].astype(BF16), w2=w2[0].astype(BF16),
             final_g=final_g[None, :])
    y_prompt = _encode(x_prompt, mod[:nb_p], p)
    y_sample = _encode(x_sample, mod[nb_p:nb_p + nb_s], p)
    return (y_prompt, y_sample)
```

```python
import functools
import math

import jax
import jax.numpy as jnp
from jax import lax
from jax.experimental import pallas as pl
from jax.experimental.pallas import tpu as pltpu

F32 = jnp.float32
BF16 = jnp.bfloat16

D_MODEL = 1024
ATTN_WIDTH = 512
SSM_WIDTH = 512
HEAD_DIM = 64
PAIR = 2 * HEAD_DIM
N_PAIRS = ATTN_WIDTH // PAIR
DILATIONS = (1, 4, 16)
HALF_KEYS = 64
Q_TILE = 128
K_WIN = Q_TILE + 2 * HALF_KEYS
ROPE_THETA = 10000.0
SSM_GROUP = 16
N_GROUPS = SSM_WIDTH // SSM_GROUP
SSM_STATE = 64
CHUNK = 16
BUNDLE_GROUPS = 8
N_BUNDLES = N_GROUPS // BUNDLE_GROUPS
BUNDLE_W = CHUNK * BUNDLE_GROUPS * SSM_GROUP
STATE_W = BUNDLE_GROUPS * SSM_STATE
FFN_HIDDEN = 2816
N_MOD = 6
EPS = 1e-6
NEG_BIG = -1e30
VMEM_LIMIT = 56 * 1024 * 1024


def _cparams(sem):
    return pltpu.CompilerParams(dimension_semantics=sem, vmem_limit_bytes=VMEM_LIMIT)


def _const_spec(shape):
    nd = len(shape)
    return pl.BlockSpec(shape, lambda *_: (0,) * nd, pipeline_mode=pl.Buffered(1))


def _sigmoid(x):
    return 1.0 / (1.0 + jnp.exp(-x))


def _rms(x, g):
    return x * lax.rsqrt(jnp.mean(x * x, axis=-1, keepdims=True) + EPS) * g


def _mod_kernel(c_ref, w_ref, b_ref, o_ref):
    c = c_ref[...]
    s = c * _sigmoid(c)
    o_ref[...] = jnp.dot(s.astype(BF16), w_ref[...].astype(BF16),
                         preferred_element_type=F32) + b_ref[...]


def _mod_call(c_all, w_ada, b_ada):
    rows = c_all.shape[0]
    n_out = w_ada.shape[1]
    tn = 1024
    return pl.pallas_call(
        _mod_kernel,
        grid=(n_out // tn,),
        in_specs=[pl.BlockSpec((rows, D_MODEL), lambda j: (0, 0)),
                  pl.BlockSpec((D_MODEL, tn), lambda j: (0, j)),
                  pl.BlockSpec((1, tn), lambda j: (0, j))],
        out_specs=pl.BlockSpec((rows, tn), lambda j: (0, j)),
        out_shape=jax.ShapeDtypeStruct((rows, n_out), F32),
        compiler_params=_cparams(("arbitrary",)),
        name="adaln_mod",
    )(c_all, w_ada, b_ada)


def _proj_kernel(x_ref, mod_ref, g_ref, w_ref, cos_ref, sin_ref,
                 q1_ref, k1_ref, v1_ref, q4_ref, k4_ref, v4_ref, q16_ref, k16_ref, v16_ref,
                 u_ref, res_ref, tmp_ref, *, tm):
    x = x_ref[...]
    h = _rms(x, g_ref[...]) * (1.0 + mod_ref[1:2, :]) + mod_ref[0:1, :]
    p = jnp.dot(h.astype(BF16), w_ref[...], preferred_element_type=F32)
    cos = jnp.concatenate([cos_ref[...]] * N_PAIRS, axis=1)
    sin = jnp.concatenate([sin_ref[...]] * N_PAIRS, axis=1)
    lane = lax.broadcasted_iota(jnp.int32, (1, ATTN_WIDTH), 1)
    first_half = (lane % HEAD_DIM) < (HEAD_DIM // 2)

    def rope(t):
        fwd = pltpu.roll(t, ATTN_WIDTH - HEAD_DIM // 2, 1)
        bwd = pltpu.roll(t, HEAD_DIM // 2, 1)
        return t * cos + jnp.where(first_half, fwd, bwd) * sin

    qkv = (rope(p[:, 0:ATTN_WIDTH]) * (HEAD_DIM ** -0.5),
           rope(p[:, ATTN_WIDTH:2 * ATTN_WIDTH]),
           p[:, 2 * ATTN_WIDTH:3 * ATTN_WIDTH])
    u_ref[...] = p[:, 3 * ATTN_WIDTH:]
    for a, (t, ref) in enumerate(zip(qkv, (q1_ref, k1_ref, v1_ref))):
        ref[...] = t.astype(BF16)
        for c in range(N_PAIRS):
            res_ref[a * N_PAIRS + c] = t[:, c * PAIR:(c + 1) * PAIR]
    n4, n16 = tm // 4, tm // 16
    for a, (ref4, ref16) in enumerate(((q4_ref, q16_ref), (k4_ref, k16_ref), (v4_ref, v16_ref))):
        for c in range(N_PAIRS):
            slab = a * N_PAIRS + c
            cols = slice(c * PAIR, (c + 1) * PAIR)
            for r4 in range(4):
                t4 = res_ref[slab, pl.ds(r4, n4, stride=4), :]
                ref4[r4, :, cols] = t4.astype(BF16)
                tmp_ref[slab, r4 * n4:(r4 + 1) * n4, :] = t4
            for r4 in range(4):
                for rp in range(4):
                    piece = tmp_ref[slab, pl.ds(r4 * n4 + rp, n16, stride=4), :]
                    ref16[r4 + 4 * rp, :, cols] = piece.astype(BF16)


def _proj_call(x, mod3, norm1_g, w_in, cos_t, sin_t, tm):
    b, s, _ = x.shape
    row = lambda bi, i: (bi, i, 0)
    nat = pl.BlockSpec((None, tm, ATTN_WIDTH), row)
    out_specs, out_shape = [nat] * 3, [jax.ShapeDtypeStruct((b, s, ATTN_WIDTH), BF16)] * 3
    for d in DILATIONS[1:]:
        out_specs += [pl.BlockSpec((None, d, tm // d, ATTN_WIDTH), lambda bi, i: (bi, 0, i, 0))] * 3
        out_shape += [jax.ShapeDtypeStruct((b, d, s // d, ATTN_WIDTH), BF16)] * 3
    out_specs.append(pl.BlockSpec((None, tm, SSM_WIDTH), row))
    out_shape.append(jax.ShapeDtypeStruct((b, s, SSM_WIDTH), F32))
    return pl.pallas_call(
        functools.partial(_proj_kernel, tm=tm),
        grid=(b, s // tm),
        in_specs=[pl.BlockSpec((None, tm, D_MODEL), row),
                  pl.BlockSpec((None, N_MOD, D_MODEL), lambda bi, i: (bi, 0, 0)),
                  _const_spec((1, D_MODEL)),
                  _const_spec((D_MODEL, 4 * ATTN_WIDTH)),
                  pl.BlockSpec((tm, PAIR), lambda bi, i: (i, 0)),
                  pl.BlockSpec((tm, PAIR), lambda bi, i: (i, 0))],
        out_specs=out_specs,
        out_shape=out_shape,
        scratch_shapes=[pltpu.VMEM((3 * N_PAIRS, tm, PAIR), F32)] * 2,
        compiler_params=_cparams(("arbitrary", "arbitrary")),
        name="proj_rope",
    )(x, mod3, norm1_g, w_in, cos_t, sin_t)


def _attn_kernel(bias_ref, q_ref, k_ref, v_ref, o_ref, lse_ref, *, sub_len, tq, rr, pp):
    i = pl.program_id(3)
    lane_b = lax.broadcasted_iota(jnp.int32, (Q_TILE, PAIR), 1) < HEAD_DIM

    def one_residue(r):
        for sb in range(tq // Q_TILE):
            rows = slice(sb * Q_TILE, (sb + 1) * Q_TILE)
            m0 = i * tq + sb * Q_TILE
            start = jnp.clip(m0 - HALF_KEYS, 0, sub_len - K_WIN)
            start = pl.multiple_of(start, HALF_KEYS)
            bias = bias_ref[jnp.where(m0 == 0, 1, jnp.where(m0 == sub_len - Q_TILE, 2, 0))]
            for hp in range(pp):
                cols = slice(hp * PAIR, (hp + 1) * PAIR)
                q = q_ref[r, rows, cols]
                kw = k_ref[r, pl.ds(start, K_WIN), cols]
                vw = v_ref[r, pl.ds(start, K_WIN), cols]
                zero = jnp.zeros_like(q)
                q2 = jnp.concatenate([jnp.where(lane_b, q, zero), jnp.where(lane_b, zero, q)],
                                     axis=0)
                s = lax.dot_general(q2, kw, (((1,), (1,)), ((), ())),
                                    preferred_element_type=F32) + bias
                m = jnp.max(s, axis=1, keepdims=True)
                p = jnp.exp(s - m)
                l = jnp.sum(p, axis=1, keepdims=True)
                pv = jnp.dot(p.astype(BF16), vw, preferred_element_type=F32)
                o2 = pv / l
                lse2 = jnp.broadcast_to(m + jnp.log(l), (2 * Q_TILE, PAIR))
                o_ref[r, rows, cols] = jnp.where(lane_b, o2[:Q_TILE], o2[Q_TILE:])
                lse_ref[r, rows, cols] = jnp.where(lane_b, lse2[:Q_TILE], lse2[Q_TILE:])

    if rr == 1:
        one_residue(0)
    else:
        def body(r, carry):
            one_residue(r)
            return carry
        lax.fori_loop(0, rr, body, 0)


def _band_bias():
    row = lax.broadcasted_iota(jnp.int32, (2 * Q_TILE, K_WIN), 0) % Q_TILE
    col = lax.broadcasted_iota(jnp.int32, (2 * Q_TILE, K_WIN), 1)
    offs = jnp.array([-HALF_KEYS, 0, -2 * HALF_KEYS], jnp.int32)[:, None, None]
    valid = jnp.abs(col - row + offs) <= HALF_KEYS
    return jnp.where(valid, 0.0, NEG_BIG).astype(F32)


def _attn_call(qd, kd, vd, bias):
    b, d, sub_len, _ = qd.shape
    pp = max(1, min(N_PAIRS, (4 * 1024 * 1024) // (sub_len * PAIR * 2)))
    tq = min(512 if pp > 1 else 1024, sub_len)
    rr = max(1, min(d, 16 // (pp * (tq // Q_TILE))))
    qspec = pl.BlockSpec((None, rr, tq, pp * PAIR), lambda bi, r, hp, i: (bi, r, i, hp))
    kspec = pl.BlockSpec((None, rr, sub_len, pp * PAIR), lambda bi, r, hp, i: (bi, r, 0, hp))
    out_sd = jax.ShapeDtypeStruct((b, d, sub_len, ATTN_WIDTH), F32)
    return pl.pallas_call(
        functools.partial(_attn_kernel, sub_len=sub_len, tq=tq, rr=rr, pp=pp),
        grid=(b, d // rr, N_PAIRS // pp, sub_len // tq),
        in_specs=[_const_spec((3, 2 * Q_TILE, K_WIN)), qspec, kspec, kspec],
        out_specs=[qspec, qspec],
        out_shape=[out_sd, out_sd],
        compiler_params=_cparams(("arbitrary",) * 4),
        name=f"dilated_attn_d{d}",
    )(bias, qd, kd, vd)


def _ssm_in_kernel(u_ref, t_ref, we_ref, y_ref, e_ref, lhs_ref, *, mc):
    for j in range(CHUNK):
        lhs_ref[:, j * PAIR:(j + 1) * PAIR] = u_ref[pl.ds(j, mc, stride=CHUNK), :].astype(BF16)
    x = lhs_ref[...]
    y_ref[...] = jnp.dot(x, t_ref[...], preferred_element_type=F32)
    e_ref[...] = jnp.dot(x, we_ref[...], preferred_element_type=F32)


def _ssm_in_call(u, t_w, we_w, mc):
    b, s, _ = u.shape
    n_chunks = s // CHUNK
    tile = pl.BlockSpec((None, None, mc, BUNDLE_W), lambda bun, bi, i: (bi, bun, i, 0))
    wspec = pl.BlockSpec((None, BUNDLE_W, BUNDLE_W), lambda bun, bi, i: (bun, 0, 0),
                         pipeline_mode=pl.Buffered(1))
    out_sd = jax.ShapeDtypeStruct((b, N_BUNDLES, n_chunks, BUNDLE_W), F32)
    return pl.pallas_call(
        functools.partial(_ssm_in_kernel, mc=mc),
        grid=(N_BUNDLES, b, n_chunks // mc),
        in_specs=[pl.BlockSpec((None, CHUNK * mc, PAIR), lambda bun, bi, i: (bi, i, bun)),
                  wspec, wspec],
        out_specs=[tile, tile],
        out_shape=[out_sd, out_sd],
        scratch_shapes=[pltpu.VMEM((mc, BUNDLE_W), BF16)],
        compiler_params=_cparams(("arbitrary",) * 3),
        name="s5_chunk_in",
    )(u, t_w, we_w)


def _ssm_scan_kernel(e_ref, a_ref, h_ref, *, n_chunks):
    w = STATE_W
    coef = [(a_ref[dr, 0:1, :], a_ref[dr, 1:2, :]) for dr in range(2)]

    def body(s, carry):
        rows = (s, n_chunks - 1 - s)
        new = []
        for dr in range(2):
            ar, ai = coef[dr]
            hr, hi = carry[2 * dr], carry[2 * dr + 1]
            k = rows[dr]
            h_ref[pl.ds(k, 1), (2 * dr) * w:(2 * dr + 1) * w] = hr
            h_ref[pl.ds(k, 1), (2 * dr + 1) * w:(2 * dr + 2) * w] = hi
            er = e_ref[pl.ds(k, 1), (2 * dr) * w:(2 * dr + 1) * w]
            ei = e_ref[pl.ds(k, 1), (2 * dr + 1) * w:(2 * dr + 2) * w]
            new += [ar * hr - ai * hi + er, ar * hi + ai * hr + ei]
        return tuple(new)

    zero = jnp.zeros((1, w), F32)
    lax.fori_loop(0, n_chunks, body, (zero,) * 4, unroll=4)


def _ssm_scan_call(e, a_pow):
    b, _, n_chunks, _ = e.shape
    tile = pl.BlockSpec((None, None, n_chunks, 4 * STATE_W), lambda bi, bun: (bi, bun, 0, 0))
    return pl.pallas_call(
        functools.partial(_ssm_scan_kernel, n_chunks=n_chunks),
        grid=(b, N_BUNDLES),
        in_specs=[tile,
                  pl.BlockSpec((None, 2, 2, STATE_W), lambda bi, bun: (bun, 0, 0, 0))],
        out_specs=tile,
        out_shape=jax.ShapeDtypeStruct(e.shape, F32),
        compiler_params=_cparams(("arbitrary",) * 2),
        name="s5_chunk_scan",
    )(e, a_pow)


def _ssm_out_kernel(h_ref, v_ref, yi_ref, y_ref, *, mc):
    y = yi_ref[...] + jnp.dot(h_ref[...].astype(BF16), v_ref[...], preferred_element_type=F32)
    for t in range(CHUNK):
        y_ref[pl.ds(t, mc, stride=CHUNK), :] = y[:, t * PAIR:(t + 1) * PAIR]


def _ssm_out_call(h_in, v_w, y_intra, mc):
    b, _, n_chunks, _ = h_in.shape
    tile = pl.BlockSpec((None, None, mc, BUNDLE_W), lambda bun, bi, i: (bi, bun, i, 0))
    wspec = pl.BlockSpec((None, BUNDLE_W, BUNDLE_W), lambda bun, bi, i: (bun, 0, 0),
                         pipeline_mode=pl.Buffered(1))
    return pl.pallas_call(
        functools.partial(_ssm_out_kernel, mc=mc),
        grid=(N_BUNDLES, b, n_chunks // mc),
        in_specs=[tile, wspec, tile],
        out_specs=pl.BlockSpec((None, CHUNK * mc, PAIR), lambda bun, bi, i: (bi, i, bun)),
        out_shape=jax.ShapeDtypeStruct((b, n_chunks * CHUNK, SSM_WIDTH), F32),
        compiler_params=_cparams(("arbitrary",) * 3),
        name="s5_chunk_out",
    )(h_in, v_w, y_intra)


def _cpow(lam_re, lam_im, dt, n):
    nn = n.astype(F32).reshape(n.shape + (1, 1))
    mag = jnp.exp(nn * (lam_re * dt))
    ang = nn * (lam_im * dt)
    return mag * jnp.cos(ang), mag * jnp.sin(ang)


def _expand_kernel(c_ref, e_ref, o_ref, *, tr, row_div, col_div):
    w = jnp.dot(c_ref[...].astype(BF16), e_ref[...], preferred_element_type=F32)
    rows = lax.broadcasted_iota(jnp.int32, (tr, BUNDLE_W), 0) + pl.program_id(1) * tr
    cols = lax.broadcasted_iota(jnp.int32, (tr, BUNDLE_W), 1)
    same_group = (rows // row_div) % BUNDLE_GROUPS == (cols // col_div) % BUNDLE_GROUPS
    o_ref[...] = jnp.where(same_group, w, 0.0).astype(BF16)


def _expand_call(compact, spread, row_div, col_div, name):
    tr = 512
    return pl.pallas_call(
        functools.partial(_expand_kernel, tr=tr, row_div=row_div, col_div=col_div),
        grid=(N_BUNDLES, BUNDLE_W // tr),
        in_specs=[pl.BlockSpec((None, tr, 2 * PAIR), lambda bun, i: (bun, i, 0)),
                  pl.BlockSpec((2 * PAIR, BUNDLE_W), lambda bun, i: (0, 0))],
        out_specs=pl.BlockSpec((None, tr, BUNDLE_W), lambda bun, i: (bun, i, 0)),
        out_shape=jax.ShapeDtypeStruct((N_BUNDLES, BUNDLE_W, BUNDLE_W), BF16),
        compiler_params=_cparams(("arbitrary", "arbitrary")),
        name=name,
    )(compact, spread)


def _ssm_weights(lam_re, lam_im, log_dt, b_re, b_im, c_re, c_im):
    hi = lax.Precision.HIGHEST
    steps = jnp.arange(CHUNK)
    k_lag, w_e, v_c, a_pow = [], [], [], []
    for dr in range(2):
        lr, li = lam_re[dr], lam_im[dr]
        dt = jnp.exp(log_dt[dr])[:, None]
        a_r, a_i = _cpow(lr, li, dt, jnp.ones((), F32))
        nr = a_r - 1.0
        den = lr * lr + li * li
        z_r = ((nr * lr + a_i * li) / den)[..., None]
        z_i = ((a_i * lr - nr * li) / den)[..., None]
        bb_r = z_r * b_re[dr] - z_i * b_im[dr]
        bb_i = z_r * b_im[dr] + z_i * b_re[dr]
        cr = jnp.swapaxes(c_re[dr], 1, 2)
        ci = jnp.swapaxes(c_im[dr], 1, 2)

        p_r, p_i = _cpow(lr, li, dt, steps)
        ca_r = p_r[..., None] * cr - p_i[..., None] * ci
        ca_i = p_r[..., None] * ci + p_i[..., None] * cr
        k_lag.append(jnp.einsum('lgpc,gpd->lgdc', ca_r, bb_r, precision=hi)
                     - jnp.einsum('lgpc,gpd->lgdc', ca_i, bb_i, precision=hi))

        e_r, e_i = _cpow(lr, li, dt, (CHUNK - 1 - steps) if dr == 0 else steps)
        we_r = e_r[..., None] * bb_r - e_i[..., None] * bb_i
        we_i = e_r[..., None] * bb_i + e_i[..., None] * bb_r
        w_e.append((we_r, we_i))

        o_r, o_i = _cpow(lr, li, dt, (steps + 1) if dr == 0 else (CHUNK - steps))
        vo_r = o_r[..., None] * cr - o_i[..., None] * ci
        vo_i = o_r[..., None] * ci + o_i[..., None] * cr
        v_c.append((vo_r, -vo_i))

        a_pow.append(_cpow(lr, li, dt, jnp.full((), CHUNK, F32)))

    def bundle(x):
        return x.reshape((x.shape[0], N_BUNDLES, BUNDLE_GROUPS) + x.shape[2:])

    jj = steps[:, None]
    tt = steps[None, :]
    kf = k_lag[0][jnp.clip(tt - jj, 0, CHUNK - 1)] * (tt >= jj)[..., None, None, None].astype(F32)
    kb = k_lag[1][jnp.clip(jj - tt, 0, CHUNK - 1)] * (jj >= tt)[..., None, None, None].astype(F32)
    toe = (kf + kb).reshape(CHUNK, CHUNK, N_BUNDLES, BUNDLE_GROUPS, SSM_GROUP, SSM_GROUP)
    t_c = jnp.transpose(toe, (2, 0, 3, 4, 1, 5)).reshape(N_BUNDLES, BUNDLE_W, 2 * PAIR)

    quarters = [w_e[0][0], w_e[0][1], w_e[1][0], w_e[1][1]]
    we = jnp.stack([bundle(x) for x in quarters], axis=0)
    we_c = jnp.transpose(we, (2, 1, 3, 5, 0, 4)).reshape(N_BUNDLES, BUNDLE_W, 2 * PAIR)

    quarters = [v_c[0][0], v_c[0][1], v_c[1][0], v_c[1][1]]
    vv = jnp.stack([bundle(x) for x in quarters], axis=0)
    v_cmp = jnp.transpose(vv, (2, 0, 3, 4, 1, 5)).reshape(N_BUNDLES, BUNDLE_W, 2 * PAIR)

    src = jnp.arange(2 * PAIR)[:, None]
    dst = jnp.arange(BUNDLE_W)[None, :]
    spread_tc = ((src // SSM_GROUP == dst // PAIR) & (src % SSM_GROUP == dst % SSM_GROUP)).astype(BF16)
    spread_qp = ((src // SSM_STATE == dst // STATE_W) & (src % SSM_STATE == dst % SSM_STATE)).astype(BF16)

    t_w = _expand_call(t_c, spread_tc, SSM_GROUP, SSM_GROUP, "s5_expand_toeplitz")
    we_w = _expand_call(we_c, spread_qp, SSM_GROUP, SSM_STATE, "s5_expand_inject")
    v_w = _expand_call(v_cmp, spread_tc, SSM_STATE, SSM_GROUP, "s5_expand_carry")

    ap = jnp.stack([jnp.stack([a_pow[dr][0], a_pow[dr][1]], axis=0) for dr in range(2)], axis=0)
    ap = ap.reshape(2, 2, N_BUNDLES, STATE_W)
    ap = jnp.transpose(ap, (2, 0, 1, 3))
    return t_w, we_w, v_w, ap


def _mix_kernel(x_ref, mod_ref, o1_ref, l1_ref, o4_ref, l4_ref, o16_ref, l16_ref,
                y_ref, u_ref, dskip_ref, wglu_ref, bglu_ref, ang_ref, sng_ref,
                wo_ref, x1_ref, nat_ref, tmp_ref, *, tm):
    n4, n16 = tm // 4, tm // 16

    def natural(ref, d, base):
        for c in range(N_PAIRS):
            cols = slice(c * PAIR, (c + 1) * PAIR)
            for r4 in range(4):
                if d == 16:
                    for rp in range(4):
                        tmp_ref[c, pl.ds(r4 * n4 + rp, n16, stride=4), :] = ref[r4 + 4 * rp, :, cols]
                    quarter = tmp_ref[c, r4 * n4:(r4 + 1) * n4, :]
                else:
                    quarter = ref[r4, :, cols]
                nat_ref[base + c, pl.ds(r4, n4, stride=4), :] = quarter
        return jnp.concatenate([nat_ref[base + c] for c in range(N_PAIRS)], axis=1)

    o2, l2 = natural(o4_ref, 4, 0), natural(l4_ref, 4, N_PAIRS)
    o3, l3 = natural(o16_ref, 16, 2 * N_PAIRS), natural(l16_ref, 16, 3 * N_PAIRS)
    l1 = l1_ref[...]
    lm = jnp.maximum(jnp.maximum(l1, l2), l3)
    e1, e2, e3 = jnp.exp(l1 - lm), jnp.exp(l2 - lm), jnp.exp(l3 - lm)
    attn = (e1 * o1_ref[...] + e2 * o2 + e3 * o3) / (e1 + e2 + e3)
    an = _rms(attn, ang_ref[...])

    y = y_ref[...] + dskip_ref[...] * u_ref[...]
    g = 0.5 * y * (1.0 + jnp.tanh(math.sqrt(2.0 / math.pi) * (y + 0.044715 * (y * y * y))))
    z = jnp.dot(g.astype(BF16), wglu_ref[...], preferred_element_type=F32) + bglu_ref[...]
    sn = _rms(g * _sigmoid(z), sng_ref[...])

    mixed = (jnp.dot(an.astype(BF16), wo_ref[0:ATTN_WIDTH, :], preferred_element_type=F32)
             + jnp.dot(sn.astype(BF16), wo_ref[ATTN_WIDTH:, :], preferred_element_type=F32))
    x1_ref[...] = x_ref[...] + mod_ref[2:3, :] * mixed


def _mix_call(x, mod3, branch_outs, y_ssm, u, d_skip, w_glu, b_glu, attn_g, ssm_g, w_o, tm):
    b, s, _ = x.shape
    row = lambda bi, i: (bi, i, 0)
    wide = pl.BlockSpec((None, tm, D_MODEL), row)
    half = pl.BlockSpec((None, tm, ATTN_WIDTH), row)
    res = lambda d: pl.BlockSpec((None, d, tm // d, ATTN_WIDTH), lambda bi, i: (bi, 0, i, 0))
    return pl.pallas_call(
        functools.partial(_mix_kernel, tm=tm),
        grid=(b, s // tm),
        in_specs=[wide, pl.BlockSpec((None, N_MOD, D_MODEL), lambda bi, i: (bi, 0, 0)),
                  half, half, res(4), res(4), res(16), res(16), half, half,
                  _const_spec((1, SSM_WIDTH)), _const_spec((SSM_WIDTH, SSM_WIDTH)),
                  _const_spec((1, SSM_WIDTH)), _const_spec((1, ATTN_WIDTH)),
                  _const_spec((1, SSM_WIDTH)), _const_spec((D_MODEL, D_MODEL))],
        out_specs=wide,
        out_shape=jax.ShapeDtypeStruct((b, s, D_MODEL), F32),
        scratch_shapes=[pltpu.VMEM((4 * N_PAIRS, tm, PAIR), F32),
                        pltpu.VMEM((N_PAIRS, tm, PAIR), F32)],
        compiler_params=_cparams(("arbitrary", "arbitrary")),
        name="merge_mix",
    )(x, mod3, *branch_outs, y_ssm, u, d_skip, w_glu, b_glu, attn_g, ssm_g, w_o)


FFN_CHUNK = 256


def _ffn_kernel(x1_ref, mod_ref, n2g_ref, w1_ref, w3_ref, w2_ref, fg_ref, out_ref, acc_ref):
    x1 = x1_ref[...]
    h = (_rms(x1, n2g_ref[...]) * (1.0 + mod_ref[4:5, :]) + mod_ref[3:4, :]).astype(BF16)
    for c in range(FFN_HIDDEN // FFN_CHUNK):
        cols = slice(c * FFN_CHUNK, (c + 1) * FFN_CHUNK)
        a = jnp.dot(h, w1_ref[:, cols], preferred_element_type=F32)
        bgate = jnp.dot(h, w3_ref[:, cols], preferred_element_type=F32)
        hid = (a * _sigmoid(a) * bgate).astype(BF16)
        part = jnp.dot(hid, w2_ref[cols, :], preferred_element_type=F32)
        if c == 0:
            acc_ref[...] = part
        else:
            acc_ref[...] += part
    x2 = x1 + mod_ref[5:6, :] * acc_ref[...]
    out_ref[...] = _rms(x2, fg_ref[...])


def _ffn_call(x1, mod3, norm2_g, w1, w3, w2, final_g, tm):
    b, s, _ = x1.shape
    wide = pl.BlockSpec((None, tm, D_MODEL), lambda bi, i: (bi, i, 0))
    return pl.pallas_call(
        _ffn_kernel,
        grid=(b, s // tm),
        in_specs=[wide, pl.BlockSpec((None, N_MOD, D_MODEL), lambda bi, i: (bi, 0, 0)),
                  _const_spec((1, D_MODEL)), _const_spec((D_MODEL, FFN_HIDDEN)),
                  _const_spec((D_MODEL, FFN_HIDDEN)), _const_spec((FFN_HIDDEN, D_MODEL)),
                  _const_spec((1, D_MODEL))],
        out_specs=wide,
        out_shape=jax.ShapeDtypeStruct((b, s, D_MODEL), F32),
        scratch_shapes=[pltpu.VMEM((tm, D_MODEL), F32)],
        compiler_params=_cparams(("arbitrary", "arbitrary")),
        name="swiglu_ffn",
    )(x1, mod3, norm2_g, w1, w3, w2, final_g)


def _rope_tables(seq_len):
    lane = jnp.arange(PAIR)
    inv = 1.0 / (ROPE_THETA ** ((2 * (lane % (HEAD_DIM // 2))).astype(F32) / HEAD_DIM))
    sign = jnp.where(lane % HEAD_DIM < HEAD_DIM // 2, -1.0, 1.0).astype(F32)
    ang = jnp.arange(seq_len, dtype=F32)[:, None] * inv[None, :]
    return jnp.cos(ang), jnp.sin(ang) * sign[None, :]


def _encode(x, mod3, p):
    b, s, _ = x.shape
    n_chunks = s // CHUNK
    cos_t, sin_t = _rope_tables(s)
    (q1, k1, v1, q4, k4, v4, q16, k16, v16, u) = _proj_call(
        x, mod3, p["norm1_g"], p["w_in"], cos_t, sin_t, tm=512)

    bias = _band_bias()
    o1, l1 = _attn_call(q1[:, None], k1[:, None], v1[:, None], bias)
    o4, l4 = _attn_call(q4, k4, v4, bias)
    o16, l16 = _attn_call(q16, k16, v16, bias)
    branch_outs = (o1[:, 0], l1[:, 0], o4, l4, o16, l16)

    mc = min(512, n_chunks)
    y_intra, e = _ssm_in_call(u, p["t_w"], p["we_w"], mc)
    h_in = _ssm_scan_call(e, p["a_pow"])
    y_ssm = _ssm_out_call(h_in, p["v_w"], y_intra, mc)

    x1 = _mix_call(x, mod3, branch_outs, y_ssm, u, p["d_skip"], p["w_glu"], p["b_glu"],
                   p["attn_norm_g"], p["ssm_norm_g"], p["w_o"], tm=512)
    return _ffn_call(x1, mod3, p["norm2_g"], p["w1"], p["w3"], p["w2"], p["final_g"], tm=1024)


def kernel(x_prompt, x_sample, c_prompt, c_sample, w_ada, b_ada, norm1_g, w_in, lam_re, lam_im,
           log_dt, b_re, b_im, c_re, c_im, d_skip, w_glu, b_glu, attn_norm_g, ssm_norm_g, w_o,
           norm2_g, w1, w3, w2, final_g):
    nb_p, nb_s = c_prompt.shape[0], c_sample.shape[0]
    rows = -(-(nb_p + nb_s) // 8) * 8
    c_all = jnp.concatenate([c_prompt, c_sample,
                             jnp.zeros((rows - nb_p - nb_s, D_MODEL), F32)], axis=0)
    mod = _mod_call(c_all, w_ada[0], b_ada[0][None, :]).reshape(rows, N_MOD, D_MODEL)

    t_w, we_w, v_w, a_pow = _ssm_weights(lam_re[0], lam_im[0], log_dt[0], b_re[0], b_im[0],
                                         c_re[0], c_im[0])
    p = dict(norm1_g=norm1_g[0][None, :], w_in=w_in[0].astype(BF16),
             t_w=t_w, we_w=we_w, v_w=v_w, a_pow=a_pow,
             d_skip=d_skip[0][None, :], w_glu=w_glu[0].astype(BF16), b_glu=b_glu[0][None, :],
             attn_norm_g=attn_norm_g[0][None, :], ssm_norm_g=ssm_norm_g[0][None, :],
             w_o=w_o[0].astype(BF16), norm2_g=norm2_g[0][None, :],
             w1=w1[0].astype(BF16), w3=w3[0].astype(BF16), w2=w2[0].astype(BF16),
             final_g=final_g[None, :])
    y_prompt = _encode(x_prompt, mod[:nb_p], p)
    y_sample = _encode(x_sample, mod[nb_p:nb_p + nb_s], p)
    return (y_prompt, y_sample)
```

```python
import functools
import math

import jax
import jax.numpy as jnp
from jax import lax
from jax.experimental import pallas as pl
from jax.experimental.pallas import tpu as pltpu

F32 = jnp.float32
BF16 = jnp.bfloat16

D_MODEL = 1024
ATTN_WIDTH = 512
SSM_WIDTH = 512
HEAD_DIM = 64
PAIR = 2 * HEAD_DIM
N_PAIRS = ATTN_WIDTH // PAIR
DILATIONS = (1, 4, 16)
HALF_KEYS = 64
Q_TILE = 128
K_WIN = Q_TILE + 2 * HALF_KEYS
ROPE_THETA = 10000.0
SSM_GROUP = 16
N_GROUPS = SSM_WIDTH // SSM_GROUP
SSM_STATE = 64
CHUNK = 16
BUNDLE_GROUPS = 8
N_BUNDLES = N_GROUPS // BUNDLE_GROUPS
BUNDLE_W = CHUNK * BUNDLE_GROUPS * SSM_GROUP
STATE_W = BUNDLE_GROUPS * SSM_STATE
FFN_HIDDEN = 2816
N_MOD = 6
EPS = 1e-6
NEG_BIG = -1e30
VMEM_LIMIT = 56 * 1024 * 1024


def _cparams(sem):
    return pltpu.CompilerParams(dimension_semantics=sem, vmem_limit_bytes=VMEM_LIMIT)


def _const_spec(shape):
    nd = len(shape)
    return pl.BlockSpec(shape, lambda *_: (0,) * nd, pipeline_mode=pl.Buffered(1))


def _sigmoid(x):
    return 1.0 / (1.0 + jnp.exp(-x))


def _rms(x, g):
    return x * lax.rsqrt(jnp.mean(x * x, axis=-1, keepdims=True) + EPS) * g


def _mod_kernel(c_ref, w_ref, b_ref, o_ref):
    c = c_ref[...]
    s = c * _sigmoid(c)
    o_ref[...] = jnp.dot(s.astype(BF16), w_ref[...].astype(BF16),
                         preferred_element_type=F32) + b_ref[...]


def _mod_call(c_all, w_ada, b_ada):
    rows = c_all.shape[0]
    n_out = w_ada.shape[1]
    tn = 1024
    return pl.pallas_call(
        _mod_kernel,
        grid=(n_out // tn,),
        in_specs=[pl.BlockSpec((rows, D_MODEL), lambda j: (0, 0)),
                  pl.BlockSpec((D_MODEL, tn), lambda j: (0, j)),
                  pl.BlockSpec((1, tn), lambda j: (0, j))],
        out_specs=pl.BlockSpec((rows, tn), lambda j: (0, j)),
        out_shape=jax.ShapeDtypeStruct((rows, n_out), F32),
        compiler_params=_cparams(("arbitrary",)),
        name="adaln_mod",
    )(c_all, w_ada, b_ada)


def _proj_kernel(x_ref, mod_ref, g_ref, w_ref, cos_ref, sin_ref,
                 q1_ref, k1_ref, v1_ref, q4_ref, k4_ref, v4_ref, q16_ref, k16_ref, v16_ref,
                 u_ref, res_ref, tmp_ref, *, tm):
    x = x_ref[...]
    h = _rms(x, g_ref[...]) * (1.0 + mod_ref[1:2, :]) + mod_ref[0:1, :]
    p = jnp.dot(h.astype(BF16), w_ref[...], preferred_element_type=F32)
    cos = jnp.concatenate([cos_ref[...]] * N_PAIRS, axis=1)
    sin = jnp.concatenate([sin_ref[...]] * N_PAIRS, axis=1)
    lane = lax.broadcasted_iota(jnp.int32, (1, ATTN_WIDTH), 1)
    first_half = (lane % HEAD_DIM) < (HEAD_DIM // 2)

    def rope(t):
        fwd = pltpu.roll(t, ATTN_WIDTH - HEAD_DIM // 2, 1)
        bwd = pltpu.roll(t, HEAD_DIM // 2, 1)
        return t * cos + jnp.where(first_half, fwd, bwd) * sin

    qkv = (rope(p[:, 0:ATTN_WIDTH]) * (HEAD_DIM ** -0.5),
           rope(p[:, ATTN_WIDTH:2 * ATTN_WIDTH]),
           p[:, 2 * ATTN_WIDTH:3 * ATTN_WIDTH])
    u_ref[...] = p[:, 3 * ATTN_WIDTH:]
    for a, (t, ref) in enumerate(zip(qkv, (q1_ref, k1_ref, v1_ref))):
        ref[...] = t.astype(BF16)
        for c in range(N_PAIRS):
            res_ref[a * N_PAIRS + c] = t[:, c * PAIR:(c + 1) * PAIR]
    n4, n16 = tm // 4, tm // 16
    for a, (ref4, ref16) in enumerate(((q4_ref, q16_ref), (k4_ref, k16_ref), (v4_ref, v16_ref))):
        for c in range(N_PAIRS):
            slab = a * N_PAIRS + c
            cols = slice(c * PAIR, (c + 1) * PAIR)
            for r4 in range(4):
                t4 = res_ref[slab, pl.ds(r4, n4, stride=4), :]
                ref4[r4, :, cols] = t4.astype(BF16)
                tmp_ref[slab, r4 * n4:(r4 + 1) * n4, :] = t4
            for r4 in range(4):
                for rp in range(4):
                    piece = tmp_ref[slab, pl.ds(r4 * n4 + rp, n16, stride=4), :]
                    ref16[r4 + 4 * rp, :, cols] = piece.astype(BF16)


def _proj_call(x, mod3, norm1_g, w_in, cos_t, sin_t, tm):
    b, s, _ = x.shape
    row = lambda bi, i: (bi, i, 0)
    nat = pl.BlockSpec((None, tm, ATTN_WIDTH), row)
    out_specs, out_shape = [nat] * 3, [jax.ShapeDtypeStruct((b, s, ATTN_WIDTH), BF16)] * 3
    for d in DILATIONS[1:]:
        out_specs += [pl.BlockSpec((None, d, tm // d, ATTN_WIDTH), lambda bi, i: (bi, 0, i, 0))] * 3
        out_shape += [jax.ShapeDtypeStruct((b, d, s // d, ATTN_WIDTH), BF16)] * 3
    out_specs.append(pl.BlockSpec((None, tm, SSM_WIDTH), row))
    out_shape.append(jax.ShapeDtypeStruct((b, s, SSM_WIDTH), F32))
    return pl.pallas_call(
        functools.partial(_proj_kernel, tm=tm),
        grid=(b, s // tm),
        in_specs=[pl.BlockSpec((None, tm, D_MODEL), row),
                  pl.BlockSpec((None, N_MOD, D_MODEL), lambda bi, i: (bi, 0, 0)),
                  _const_spec((1, D_MODEL)),
                  _const_spec((D_MODEL, 4 * ATTN_WIDTH)),
                  pl.BlockSpec((tm, PAIR), lambda bi, i: (i, 0)),
                  pl.BlockSpec((tm, PAIR), lambda bi, i: (i, 0))],
        out_specs=out_specs,
        out_shape=out_shape,
        scratch_shapes=[pltpu.VMEM((3 * N_PAIRS, tm, PAIR), F32)] * 2,
        compiler_params=_cparams(("arbitrary", "arbitrary")),
        name="proj_rope",
    )(x, mod3, norm1_g, w_in, cos_t, sin_t)


def _attn_kernel(bias_ref, q_ref, k_ref, v_ref, o_ref, lse_ref, *, sub_len, tq, rr, pp):
    i = pl.program_id(3)
    lane_b = lax.broadcasted_iota(jnp.int32, (Q_TILE, PAIR), 1) < HEAD_DIM

    def one_residue(r):
        for sb in range(tq // Q_TILE):
            rows = slice(sb * Q_TILE, (sb + 1) * Q_TILE)
            m0 = i * tq + sb * Q_TILE
            start = jnp.clip(m0 - HALF_KEYS, 0, sub_len - K_WIN)
            start = pl.multiple_of(start, HALF_KEYS)
            bias = bias_ref[jnp.where(m0 == 0, 1, jnp.where(m0 == sub_len - Q_TILE, 2, 0))]
            for hp in range(pp):
                cols = slice(hp * PAIR, (hp + 1) * PAIR)
                q = q_ref[r, rows, cols]
                kw = k_ref[r, pl.ds(start, K_WIN), cols]
                vw = v_ref[r, pl.ds(start, K_WIN), cols]
                zero = jnp.zeros_like(q)
                q2 = jnp.concatenate([jnp.where(lane_b, q, zero), jnp.where(lane_b, zero, q)],
                                     axis=0)
                s = lax.dot_general(q2, kw, (((1,), (1,)), ((), ())),
                                    preferred_element_type=F32) + bias
                m = jnp.max(s, axis=1, keepdims=True)
                p = jnp.exp(s - m)
                l = jnp.sum(p, axis=1, keepdims=True)
                pv = jnp.dot(p.astype(BF16), vw, preferred_element_type=F32)
                o2 = pv / l
                lse2 = jnp.broadcast_to(m + jnp.log(l), (2 * Q_TILE, PAIR))
                o_ref[r, rows, cols] = jnp.where(lane_b, o2[:Q_TILE], o2[Q_TILE:])
                lse_ref[r, rows, cols] = jnp.where(lane_b, lse2[:Q_TILE], lse2[Q_TILE:])

    if rr == 1:
        one_residue(0)
    else:
        def body(r, carry):
            one_residue(r)
            return carry
        lax.fori_loop(0, rr, body, 0)


def _band_bias():
    row = lax.broadcasted_iota(jnp.int32, (2 * Q_TILE, K_WIN), 0) % Q_TILE
    col = lax.broadcasted_iota(jnp.int32, (2 * Q_TILE, K_WIN), 1)
    offs = jnp.array([-HALF_KEYS, 0, -2 * HALF_KEYS], jnp.int32)[:, None, None]
    valid = jnp.abs(col - row + offs) <= HALF_KEYS
    return jnp.where(valid, 0.0, NEG_BIG).astype(F32)


def _attn_call(qd, kd, vd, bias):
    b, d, sub_len, _ = qd.shape
    pp = max(1, min(N_PAIRS, (4 * 1024 * 1024) // (sub_len * PAIR * 2)))
    tq = min(512 if pp > 1 else 1024, sub_len)
    rr = max(1, min(d, 16 // (pp * (tq // Q_TILE))))
    qspec = pl.BlockSpec((None, rr, tq, pp * PAIR), lambda bi, r, hp, i: (bi, r, i, hp))
    kspec = pl.BlockSpec((None, rr, sub_len, pp * PAIR), lambda bi, r, hp, i: (bi, r, 0, hp))
    out_sd = jax.ShapeDtypeStruct((b, d, sub_len, ATTN_WIDTH), F32)
    return pl.pallas_call(
        functools.partial(_attn_kernel, sub_len=sub_len, tq=tq, rr=rr, pp=pp),
        grid=(b, d // rr, N_PAIRS // pp, sub_len // tq),
        in_specs=[_const_spec((3, 2 * Q_TILE, K_WIN)), qspec, kspec, kspec],
        out_specs=[qspec, qspec],
        out_shape=[out_sd, out_sd],
        compiler_params=_cparams(("arbitrary",) * 4),
        name=f"dilated_attn_d{d}",
    )(bias, qd, kd, vd)


def _ssm_in_kernel(u_ref, t_ref, we_ref, y_ref, e_ref, lhs_ref, *, mc):
    for j in range(CHUNK):
        lhs_ref[:, j * PAIR:(j + 1) * PAIR] = u_ref[pl.ds(j, mc, stride=CHUNK), :].astype(BF16)
    x = lhs_ref[...]
    y_ref[...] = jnp.dot(x, t_ref[...], preferred_element_type=F32)
    e_ref[...] = jnp.dot(x, we_ref[...], preferred_element_type=F32)


def _ssm_in_call(u, t_w, we_w, mc):
    b, s, _ = u.shape
    n_chunks = s // CHUNK
    tile = pl.BlockSpec((None, None, mc, BUNDLE_W), lambda bun, bi, i: (bi, bun, i, 0))
    wspec = pl.BlockSpec((None, BUNDLE_W, BUNDLE_W), lambda bun, bi, i: (bun, 0, 0),
                         pipeline_mode=pl.Buffered(1))
    out_sd = jax.ShapeDtypeStruct((b, N_BUNDLES, n_chunks, BUNDLE_W), F32)
    return pl.pallas_call(
        functools.partial(_ssm_in_kernel, mc=mc),
        grid=(N_BUNDLES, b, n_chunks // mc),
        in_specs=[pl.BlockSpec((None, CHUNK * mc, PAIR), lambda bun, bi, i: (bi, i, bun)),
                  wspec, wspec],
        out_specs=[tile, tile],
        out_shape=[out_sd, out_sd],
        scratch_shapes=[pltpu.VMEM((mc, BUNDLE_W), BF16)],
        compiler_params=_cparams(("arbitrary",) * 3),
        name="s5_chunk_in",
    )(u, t_w, we_w)


def _ssm_scan_kernel(e_ref, a_ref, h_ref, *, n_chunks):
    w = STATE_W
    coef = [(a_ref[dr, 0:1, :], a_ref[dr, 1:2, :]) for dr in range(2)]

    def body(s, carry):
        rows = (s, n_chunks - 1 - s)
        new = []
        for dr in range(2):
            ar, ai = coef[dr]
            hr, hi = carry[2 * dr], carry[2 * dr + 1]
            k = rows[dr]
            h_ref[pl.ds(k, 1), (2 * dr) * w:(2 * dr + 1) * w] = hr
            h_ref[pl.ds(k, 1), (2 * dr + 1) * w:(2 * dr + 2) * w] = hi
            er = e_ref[pl.ds(k, 1), (2 * dr) * w:(2 * dr + 1) * w]
            ei = e_ref[pl.ds(k, 1), (2 * dr + 1) * w:(2 * dr + 2) * w]
            new += [ar * hr - ai * hi + er, ar * hi + ai * hr + ei]
        return tuple(new)

    zero = jnp.zeros((1, w), F32)
    lax.fori_loop(0, n_chunks, body, (zero,) * 4, unroll=4)


def _ssm_scan_call(e, a_pow):
    b, _, n_chunks, _ = e.shape
    tile = pl.BlockSpec((None, None, n_chunks, 4 * STATE_W), lambda bi, bun: (bi, bun, 0, 0))
    return pl.pallas_call(
        functools.partial(_ssm_scan_kernel, n_chunks=n_chunks),
        grid=(b, N_BUNDLES),
        in_specs=[tile,
                  pl.BlockSpec((None, 2, 2, STATE_W), lambda bi, bun: (bun, 0, 0, 0))],
        out_specs=tile,
        out_shape=jax.ShapeDtypeStruct(e.shape, F32),
        compiler_params=_cparams(("arbitrary",) * 2),
        name="s5_chunk_scan",
    )(e, a_pow)


def _ssm_out_kernel(h_ref, v_ref, yi_ref, y_ref, *, mc):
    y = yi_ref[...] + jnp.dot(h_ref[...].astype(BF16), v_ref[...], preferred_element_type=F32)
    for t in range(CHUNK):
        y_ref[pl.ds(t, mc, stride=CHUNK), :] = y[:, t * PAIR:(t + 1) * PAIR]


def _ssm_out_call(h_in, v_w, y_intra, mc):
    b, _, n_chunks, _ = h_in.shape
    tile = pl.BlockSpec((None, None, mc, BUNDLE_W), lambda bun, bi, i: (bi, bun, i, 0))
    wspec = pl.BlockSpec((None, BUNDLE_W, BUNDLE_W), lambda bun, bi, i: (bun, 0, 0),
                         pipeline_mode=pl.Buffered(1))
    return pl.pallas_call(
        functools.partial(_ssm_out_kernel, mc=mc),
        grid=(N_BUNDLES, b, n_chunks // mc),
        in_specs=[tile, wspec, tile],
        out_specs=pl.BlockSpec((None, CHUNK * mc, PAIR), lambda bun, bi, i: (bi, i, bun)),
        out_shape=jax.ShapeDtypeStruct((b, n_chunks * CHUNK, SSM_WIDTH), F32),
        compiler_params=_cparams(("arbitrary",) * 3),
        name="s5_chunk_out",
    )(h_in, v_w, y_intra)


def _cpow(lam_re, lam_im, dt, n):
    nn = n.astype(F32).reshape(n.shape + (1, 1))
    mag = jnp.exp(nn * (lam_re * dt))
    ang = nn * (lam_im * dt)
    return mag * jnp.cos(ang), mag * jnp.sin(ang)


def _expand_kernel(c_ref, e_ref, o_ref, *, tr, row_div, col_div):
    w = jnp.dot(c_ref[...].astype(BF16), e_ref[...], preferred_element_type=F32)
    rows = lax.broadcasted_iota(jnp.int32, (tr, BUNDLE_W), 0) + pl.program_id(1) * tr
    cols = lax.broadcasted_iota(jnp.int32, (tr, BUNDLE_W), 1)
    same_group = (rows // row_div) % BUNDLE_GROUPS == (cols // col_div) % BUNDLE_GROUPS
    o_ref[...] = jnp.where(same_group, w, 0.0).astype(BF16)


def _expand_call(compact, spread, row_div, col_div, name):
    tr = 512
    return pl.pallas_call(
        functools.partial(_expand_kernel, tr=tr, row_div=row_div, col_div=col_div),
        grid=(N_BUNDLES, BUNDLE_W // tr),
        in_specs=[pl.BlockSpec((None, tr, 2 * PAIR), lambda bun, i: (bun, i, 0)),
                  pl.BlockSpec((2 * PAIR, BUNDLE_W), lambda bun, i: (0, 0))],
        out_specs=pl.BlockSpec((None, tr, BUNDLE_W), lambda bun, i: (bun, i, 0)),
        out_shape=jax.ShapeDtypeStruct((N_BUNDLES, BUNDLE_W, BUNDLE_W), BF16),
        compiler_params=_cparams(("arbitrary", "arbitrary")),
        name=name,
    )(compact, spread)


def _expand_toeplitz_kernel(a_ref, e_ref, o_ref, *, tr):
    rows = lax.broadcasted_iota(jnp.int32, (tr, PAIR), 0) + pl.program_id(1) * tr
    cols = lax.broadcasted_iota(jnp.int32, (tr, PAIR), 1)
    same_group = (rows // SSM_GROUP) % BUNDLE_GROUPS == cols // SSM_GROUP
    for t in range(CHUNK):
        w = jnp.dot(a_ref[t].astype(BF16), e_ref[...], preferred_element_type=F32)
        o_ref[:, t * PAIR:(t + 1) * PAIR] = jnp.where(same_group, w, 0.0).astype(BF16)


def _expand_toeplitz_call(blocks, spread):
    tr = 512
    return pl.pallas_call(
        functools.partial(_expand_toeplitz_kernel, tr=tr),
        grid=(N_BUNDLES, BUNDLE_W // tr),
        in_specs=[pl.BlockSpec((None, CHUNK, tr, SSM_GROUP), lambda bun, i: (bun, 0, i, 0)),
                  pl.BlockSpec((SSM_GROUP, PAIR), lambda bun, i: (0, 0))],
        out_specs=pl.BlockSpec((None, tr, BUNDLE_W), lambda bun, i: (bun, i, 0)),
        out_shape=jax.ShapeDtypeStruct((N_BUNDLES, BUNDLE_W, BUNDLE_W), BF16),
        compiler_params=_cparams(("arbitrary", "arbitrary")),
        name="s5_expand_toeplitz",
    )(blocks, spread)


def _ssm_weights(lam_re, lam_im, log_dt, b_re, b_im, c_re, c_im):
    hi = lax.Precision.HIGHEST
    steps = jnp.arange(CHUNK)
    k_lag, w_e, v_c, a_pow = [], [], [], []
    for dr in range(2):
        lr, li = lam_re[dr], lam_im[dr]
        dt = jnp.exp(log_dt[dr])[:, None]
        a_r, a_i = _cpow(lr, li, dt, jnp.ones((), F32))
        nr = a_r - 1.0
        den = lr * lr + li * li
        z_r = ((nr * lr + a_i * li) / den)[..., None]
        z_i = ((a_i * lr - nr * li) / den)[..., None]
        bb_r = z_r * b_re[dr] - z_i * b_im[dr]
        bb_i = z_r * b_im[dr] + z_i * b_re[dr]
        cr = jnp.swapaxes(c_re[dr], 1, 2)
        ci = jnp.swapaxes(c_im[dr], 1, 2)

        p_r, p_i = _cpow(lr, li, dt, steps)
        ca_r = p_r[..., None] * cr - p_i[..., None] * ci
        ca_i = p_r[..., None] * ci + p_i[..., None] * cr
        k_lag.append(jnp.einsum('lgpc,gpd->lgdc', ca_r, bb_r, precision=hi)
                     - jnp.einsum('lgpc,gpd->lgdc', ca_i, bb_i, precision=hi))

        e_r, e_i = _cpow(lr, li, dt, (CHUNK - 1 - steps) if dr == 0 else steps)
        we_r = e_r[..., None] * bb_r - e_i[..., None] * bb_i
        we_i = e_r[..., None] * bb_i + e_i[..., None] * bb_r
        w_e.append((we_r, we_i))

        o_r, o_i = _cpow(lr, li, dt, (steps + 1) if dr == 0 else (CHUNK - steps))
        vo_r = o_r[..., None] * cr - o_i[..., None] * ci
        vo_i = o_r[..., None] * ci + o_i[..., None] * cr
        v_c.append((vo_r, -vo_i))

        a_pow.append(_cpow(lr, li, dt, jnp.full((), CHUNK, F32)))

    def bundle(x):
        return x.reshape((x.shape[0], N_BUNDLES, BUNDLE_GROUPS) + x.shape[2:])

    kf, kb = k_lag
    k_desc = jnp.concatenate([kf[:0:-1], (kf[0] + kb[0])[None], kb[1:]], axis=0)
    toe = jnp.stack([k_desc[CHUNK - 1 - t:2 * CHUNK - 1 - t] for t in range(CHUNK)], axis=0)
    toe = toe.reshape(CHUNK, CHUNK, N_BUNDLES, BUNDLE_GROUPS, SSM_GROUP, SSM_GROUP)
    t_c = jnp.transpose(toe, (2, 0, 1, 3, 4, 5)).reshape(N_BUNDLES, CHUNK, BUNDLE_W, SSM_GROUP)

    quarters = [w_e[0][0], w_e[0][1], w_e[1][0], w_e[1][1]]
    we = jnp.stack([bundle(x) for x in quarters], axis=0)
    we_c = jnp.transpose(we, (2, 1, 3, 5, 0, 4)).reshape(N_BUNDLES, BUNDLE_W, 2 * PAIR)

    quarters = [v_c[0][0], v_c[0][1], v_c[1][0], v_c[1][1]]
    vv = jnp.stack([bundle(x) for x in quarters], axis=0)
    v_cmp = jnp.transpose(vv, (2, 0, 3, 4, 1, 5)).reshape(N_BUNDLES, BUNDLE_W, 2 * PAIR)

    src = jnp.arange(2 * PAIR)[:, None]
    dst = jnp.arange(BUNDLE_W)[None, :]
    spread_tc = ((src // SSM_GROUP == dst // PAIR) & (src % SSM_GROUP == dst % SSM_GROUP)).astype(BF16)
    spread_qp = ((src // SSM_STATE == dst // STATE_W) & (src % SSM_STATE == dst % SSM_STATE)).astype(BF16)

    spread_c = (jnp.arange(SSM_GROUP)[:, None] == jnp.arange(PAIR)[None, :] % SSM_GROUP).astype(BF16)
    t_w = _expand_toeplitz_call(t_c, spread_c)
    we_w = _expand_call(we_c, spread_qp, SSM_GROUP, SSM_STATE, "s5_expand_inject")
    v_w = _expand_call(v_cmp, spread_tc, SSM_STATE, SSM_GROUP, "s5_expand_carry")

    ap = jnp.stack([jnp.stack([a_pow[dr][0], a_pow[dr][1]], axis=0) for dr in range(2)], axis=0)
    ap = ap.reshape(2, 2, N_BUNDLES, STATE_W)
    ap = jnp.transpose(ap, (2, 0, 1, 3))
    return t_w, we_w, v_w, ap


def _mix_kernel(x_ref, mod_ref, o1_ref, l1_ref, o4_ref, l4_ref, o16_ref, l16_ref,
                y_ref, u_ref, dskip_ref, wglu_ref, bglu_ref, ang_ref, sng_ref,
                wo_ref, x1_ref, nat_ref, tmp_ref, *, tm):
    n4, n16 = tm // 4, tm // 16

    def natural(ref, d, base):
        for c in range(N_PAIRS):
            cols = slice(c * PAIR, (c + 1) * PAIR)
            for r4 in range(4):
                if d == 16:
                    for rp in range(4):
                        tmp_ref[c, pl.ds(r4 * n4 + rp, n16, stride=4), :] = ref[r4 + 4 * rp, :, cols]
                    quarter = tmp_ref[c, r4 * n4:(r4 + 1) * n4, :]
                else:
                    quarter = ref[r4, :, cols]
                nat_ref[base + c, pl.ds(r4, n4, stride=4), :] = quarter
        return jnp.concatenate([nat_ref[base + c] for c in range(N_PAIRS)], axis=1)

    o2, l2 = natural(o4_ref, 4, 0), natural(l4_ref, 4, N_PAIRS)
    o3, l3 = natural(o16_ref, 16, 2 * N_PAIRS), natural(l16_ref, 16, 3 * N_PAIRS)
    l1 = l1_ref[...]
    lm = jnp.maximum(jnp.maximum(l1, l2), l3)
    e1, e2, e3 = jnp.exp(l1 - lm), jnp.exp(l2 - lm), jnp.exp(l3 - lm)
    attn = (e1 * o1_ref[...] + e2 * o2 + e3 * o3) / (e1 + e2 + e3)
    an = _rms(attn, ang_ref[...])

    y = y_ref[...] + dskip_ref[...] * u_ref[...]
    g = 0.5 * y * (1.0 + jnp.tanh(math.sqrt(2.0 / math.pi) * (y + 0.044715 * (y * y * y))))
    z = jnp.dot(g.astype(BF16), wglu_ref[...], preferred_element_type=F32) + bglu_ref[...]
    sn = _rms(g * _sigmoid(z), sng_ref[...])

    mixed = (jnp.dot(an.astype(BF16), wo_ref[0:ATTN_WIDTH, :], preferred_element_type=F32)
             + jnp.dot(sn.astype(BF16), wo_ref[ATTN_WIDTH:, :], preferred_element_type=F32))
    x1_ref[...] = x_ref[...] + mod_ref[2:3, :] * mixed


def _mix_call(x, mod3, branch_outs, y_ssm, u, d_skip, w_glu, b_glu, attn_g, ssm_g, w_o, tm):
    b, s, _ = x.shape
    row = lambda bi, i: (bi, i, 0)
    wide = pl.BlockSpec((None, tm, D_MODEL), row)
    half = pl.BlockSpec((None, tm, ATTN_WIDTH), row)
    res = lambda d: pl.BlockSpec((None, d, tm // d, ATTN_WIDTH), lambda bi, i: (bi, 0, i, 0))
    return pl.pallas_call(
        functools.partial(_mix_kernel, tm=tm),
        grid=(b, s // tm),
        in_specs=[wide, pl.BlockSpec((None, N_MOD, D_MODEL), lambda bi, i: (bi, 0, 0)),
                  half, half, res(4), res(4), res(16), res(16), half, half,
                  _const_spec((1, SSM_WIDTH)), _const_spec((SSM_WIDTH, SSM_WIDTH)),
                  _const_spec((1, SSM_WIDTH)), _const_spec((1, ATTN_WIDTH)),
                  _const_spec((1, SSM_WIDTH)), _const_spec((D_MODEL, D_MODEL))],
        out_specs=wide,
        out_shape=jax.ShapeDtypeStruct((b, s, D_MODEL), F32),
        scratch_shapes=[pltpu.VMEM((4 * N_PAIRS, tm, PAIR), F32),
                        pltpu.VMEM((N_PAIRS, tm, PAIR), F32)],
        compiler_params=_cparams(("arbitrary", "arbitrary")),
        name="merge_mix",
    )(x, mod3, *branch_outs, y_ssm, u, d_skip, w_glu, b_glu, attn_g, ssm_g, w_o)


FFN_CHUNK = 256


def _ffn_kernel(x1_ref, mod_ref, n2g_ref, w1_ref, w3_ref, w2_ref, fg_ref, out_ref, acc_ref):
    x1 = x1_ref[...]
    h = (_rms(x1, n2g_ref[...]) * (1.0 + mod_ref[4:5, :]) + mod_ref[3:4, :]).astype(BF16)
    for c in range(FFN_HIDDEN // FFN_CHUNK):
        cols = slice(c * FFN_CHUNK, (c + 1) * FFN_CHUNK)
        a = jnp.dot(h, w1_ref[:, cols], preferred_element_type=F32)
        bgate = jnp.dot(h, w3_ref[:, cols], preferred_element_type=F32)
        hid = (a * _sigmoid(a) * bgate).astype(BF16)
        part = jnp.dot(hid, w2_ref[cols, :], preferred_element_type=F32)
        if c == 0:
            acc_ref[...] = part
        else:
            acc_ref[...] += part
    x2 = x1 + mod_ref[5:6, :] * acc_ref[...]
    out_ref[...] = _rms(x2, fg_ref[...])


def _ffn_call(x1, mod3, norm2_g, w1, w3, w2, final_g, tm):
    b, s, _ = x1.shape
    wide = pl.BlockSpec((None, tm, D_MODEL), lambda bi, i: (bi, i, 0))
    return pl.pallas_call(
        _ffn_kernel,
        grid=(b, s // tm),
        in_specs=[wide, pl.BlockSpec((None, N_MOD, D_MODEL), lambda bi, i: (bi, 0, 0)),
                  _const_spec((1, D_MODEL)), _const_spec((D_MODEL, FFN_HIDDEN)),
                  _const_spec((D_MODEL, FFN_HIDDEN)), _const_spec((FFN_HIDDEN, D_MODEL)),
                  _const_spec((1, D_MODEL))],
        out_specs=wide,
        out_shape=jax.ShapeDtypeStruct((b, s, D_MODEL), F32),
        scratch_shapes=[pltpu.VMEM((tm, D_MODEL), F32)],
        compiler_params=_cparams(("arbitrary", "arbitrary")),
        name="swiglu_ffn",
    )(x1, mod3, norm2_g, w1, w3, w2, final_g)


def _rope_tables(seq_len):
    lane = jnp.arange(PAIR)
    inv = 1.0 / (ROPE_THETA ** ((2 * (lane % (HEAD_DIM // 2))).astype(F32) / HEAD_DIM))
    sign = jnp.where(lane % HEAD_DIM < HEAD_DIM // 2, -1.0, 1.0).astype(F32)
    ang = jnp.arange(seq_len, dtype=F32)[:, None] * inv[None, :]
    return jnp.cos(ang), jnp.sin(ang) * sign[None, :]


def _encode(x, mod3, p):
    b, s, _ = x.shape
    n_chunks = s // CHUNK
    cos_t, sin_t = _rope_tables(s)
    (q1, k1, v1, q4, k4, v4, q16, k16, v16, u) = _proj_call(
        x, mod3, p["norm1_g"], p["w_in"], cos_t, sin_t, tm=512)

    bias = _band_bias()
    o1, l1 = _attn_call(q1[:, None], k1[:, None], v1[:, None], bias)
    o4, l4 = _attn_call(q4, k4, v4, bias)
    o16, l16 = _attn_call(q16, k16, v16, bias)
    branch_outs = (o1[:, 0], l1[:, 0], o4, l4, o16, l16)

    mc = min(512, n_chunks)
    y_intra, e = _ssm_in_call(u, p["t_w"], p["we_w"], mc)
    h_in = _ssm_scan_call(e, p["a_pow"])
    y_ssm = _ssm_out_call(h_in, p["v_w"], y_intra, mc)

    x1 = _mix_call(x, mod3, branch_outs, y_ssm, u, p["d_skip"], p["w_glu"], p["b_glu"],
                   p["attn_norm_g"], p["ssm_norm_g"], p["w_o"], tm=512)
    return _ffn_call(x1, mod3, p["norm2_g"], p["w1"], p["w3"], p["w2"], p["final_g"], tm=1024)


def kernel(x_prompt, x_sample, c_prompt, c_sample, w_ada, b_ada, norm1_g, w_in, lam_re, lam_im,
           log_dt, b_re, b_im, c_re, c_im, d_skip, w_glu, b_glu, attn_norm_g, ssm_norm_g, w_o,
           norm2_g, w1, w3, w2, final_g):
    nb_p, nb_s = c_prompt.shape[0], c_sample.shape[0]
    rows = -(-(nb_p + nb_s) // 8) * 8
    c_all = jnp.concatenate([c_prompt, c_sample,
                             jnp.zeros((rows - nb_p - nb_s, D_MODEL), F32)], axis=0)
    mod = _mod_call(c_all, w_ada[0], b_ada[0][None, :]).reshape(rows, N_MOD, D_MODEL)

    t_w, we_w, v_w, a_pow = _ssm_weights(lam_re[0], lam_im[0], log_dt[0], b_re[0], b_im[0],
                                         c_re[0], c_im[0])
    p = dict(norm1_g=norm1_g[0][None, :], w_in=w_in[0].astype(BF16),
             t_w=t_w, we_w=we_w, v_w=v_w, a_pow=a_pow,
             d_skip=d_skip[0][None, :], w_glu=w_glu[0].astype(BF16), b_glu=b_glu[0][None, :],
             attn_norm_g=attn_norm_g[0][None, :], ssm_norm_g=ssm_norm_g[0][None, :],
             w_o=w_o[0].astype(BF16), norm2_g=norm2_g[0][None, :],
             w1=w1[0].astype(BF16), w3=w3[0].astype(BF16), w2=w2[0].astype(BF16),
             final_g=final_g[None, :])
    y_prompt = _encode(x_prompt, mod[:nb_p], p)
    y_sample = _encode(x_sample, mod[nb_p:nb_p + nb_s], p)
    return (y_prompt, y_sample)
```

```python
import functools
import math

import jax
import jax.numpy as jnp
from jax import lax
from jax.experimental import pallas as pl
from jax.experimental.pallas import tpu as pltpu

F32 = jnp.float32
BF16 = jnp.bfloat16

D_MODEL = 1024
ATTN_WIDTH = 512
SSM_WIDTH = 512
HEAD_DIM = 64
PAIR = 2 * HEAD_DIM
N_PAIRS = ATTN_WIDTH // PAIR
DILATIONS = (1, 4, 16)
HALF_KEYS = 64
Q_TILE = 128
K_WIN = Q_TILE + 2 * HALF_KEYS
ROPE_THETA = 10000.0
SSM_GROUP = 16
N_GROUPS = SSM_WIDTH // SSM_GROUP
SSM_STATE = 64
CHUNK = 16
BUNDLE_GROUPS = 8
N_BUNDLES = N_GROUPS // BUNDLE_GROUPS
BUNDLE_W = CHUNK * BUNDLE_GROUPS * SSM_GROUP
STATE_W = BUNDLE_GROUPS * SSM_STATE
FFN_HIDDEN = 2816
N_MOD = 6
EPS = 1e-6
NEG_BIG = -1e30
VMEM_LIMIT = 56 * 1024 * 1024


def _cparams(sem):
    return pltpu.CompilerParams(dimension_semantics=sem, vmem_limit_bytes=VMEM_LIMIT)


def _const_spec(shape):
    nd = len(shape)
    return pl.BlockSpec(shape, lambda *_: (0,) * nd, pipeline_mode=pl.Buffered(1))


def _sigmoid(x):
    return 1.0 / (1.0 + jnp.exp(-x))


def _rms(x, g):
    return x * lax.rsqrt(jnp.mean(x * x, axis=-1, keepdims=True) + EPS) * g


def _mod_kernel(c_ref, w_ref, b_ref, o_ref):
    c = c_ref[...]
    s = c * _sigmoid(c)
    o_ref[...] = jnp.dot(s.astype(BF16), w_ref[...].astype(BF16),
                         preferred_element_type=F32) + b_ref[...]


def _mod_call(c_all, w_ada, b_ada):
    rows = c_all.shape[0]
    n_out = w_ada.shape[1]
    tn = 1024
    return pl.pallas_call(
        _mod_kernel,
        grid=(n_out // tn,),
        in_specs=[pl.BlockSpec((rows, D_MODEL), lambda j: (0, 0)),
                  pl.BlockSpec((D_MODEL, tn), lambda j: (0, j)),
                  pl.BlockSpec((1, tn), lambda j: (0, j))],
        out_specs=pl.BlockSpec((rows, tn), lambda j: (0, j)),
        out_shape=jax.ShapeDtypeStruct((rows, n_out), F32),
        compiler_params=_cparams(("arbitrary",)),
        name="adaln_mod",
    )(c_all, w_ada, b_ada)


def _proj_kernel(x_ref, mod_ref, g_ref, w_ref, cos_ref, sin_ref,
                 q1_ref, k1_ref, v1_ref, q4_ref, k4_ref, v4_ref, q16_ref, k16_ref, v16_ref,
                 u_ref, res_ref, tmp_ref, *, tm):
    x = x_ref[...]
    h = _rms(x, g_ref[...]) * (1.0 + mod_ref[1:2, :]) + mod_ref[0:1, :]
    p = jnp.dot(h.astype(BF16), w_ref[...], preferred_element_type=F32)
    cos = jnp.concatenate([cos_ref[...]] * N_PAIRS, axis=1)
    sin = jnp.concatenate([sin_ref[...]] * N_PAIRS, axis=1)
    lane = lax.broadcasted_iota(jnp.int32, (1, ATTN_WIDTH), 1)
    first_half = (lane % HEAD_DIM) < (HEAD_DIM // 2)

    def rope(t):
        fwd = pltpu.roll(t, ATTN_WIDTH - HEAD_DIM // 2, 1)
        bwd = pltpu.roll(t, HEAD_DIM // 2, 1)
        return t * cos + jnp.where(first_half, fwd, bwd) * sin

    qkv = (rope(p[:, 0:ATTN_WIDTH]) * (HEAD_DIM ** -0.5),
           rope(p[:, ATTN_WIDTH:2 * ATTN_WIDTH]),
           p[:, 2 * ATTN_WIDTH:3 * ATTN_WIDTH])
    u_ref[...] = p[:, 3 * ATTN_WIDTH:]
    for a, (t, ref) in enumerate(zip(qkv, (q1_ref, k1_ref, v1_ref))):
        ref[...] = t.astype(BF16)
        for c in range(N_PAIRS):
            res_ref[a * N_PAIRS + c] = t[:, c * PAIR:(c + 1) * PAIR]
    n4, n16 = tm // 4, tm // 16
    for a, (ref4, ref16) in enumerate(((q4_ref, q16_ref), (k4_ref, k16_ref), (v4_ref, v16_ref))):
        for c in range(N_PAIRS):
            slab = a * N_PAIRS + c
            cols = slice(c * PAIR, (c + 1) * PAIR)
            for r4 in range(4):
                t4 = res_ref[slab, pl.ds(r4, n4, stride=4), :]
                ref4[r4, :, cols] = t4.astype(BF16)
                tmp_ref[slab, r4 * n4:(r4 + 1) * n4, :] = t4
            for r4 in range(4):
                for rp in range(4):
                    piece = tmp_ref[slab, pl.ds(r4 * n4 + rp, n16, stride=4), :]
                    ref16[r4 + 4 * rp, :, cols] = piece.astype(BF16)


def _proj_call(x, mod3, norm1_g, w_in, cos_t, sin_t, tm):
    b, s, _ = x.shape
    row = lambda bi, i: (bi, i, 0)
    nat = pl.BlockSpec((None, tm, ATTN_WIDTH), row)
    out_specs, out_shape = [nat] * 3, [jax.ShapeDtypeStruct((b, s, ATTN_WIDTH), BF16)] * 3
    for d in DILATIONS[1:]:
        out_specs += [pl.BlockSpec((None, d, tm // d, ATTN_WIDTH), lambda bi, i: (bi, 0, i, 0))] * 3
        out_shape += [jax.ShapeDtypeStruct((b, d, s // d, ATTN_WIDTH), BF16)] * 3
    out_specs.append(pl.BlockSpec((None, tm, SSM_WIDTH), row))
    out_shape.append(jax.ShapeDtypeStruct((b, s, SSM_WIDTH), F32))
    return pl.pallas_call(
        functools.partial(_proj_kernel, tm=tm),
        grid=(b, s // tm),
        in_specs=[pl.BlockSpec((None, tm, D_MODEL), row),
                  pl.BlockSpec((None, N_MOD, D_MODEL), lambda bi, i: (bi, 0, 0)),
                  _const_spec((1, D_MODEL)),
                  _const_spec((D_MODEL, 4 * ATTN_WIDTH)),
                  pl.BlockSpec((tm, PAIR), lambda bi, i: (i, 0)),
                  pl.BlockSpec((tm, PAIR), lambda bi, i: (i, 0))],
        out_specs=out_specs,
        out_shape=out_shape,
        scratch_shapes=[pltpu.VMEM((3 * N_PAIRS, tm, PAIR), F32)] * 2,
        compiler_params=_cparams(("arbitrary", "arbitrary")),
        name="proj_rope",
    )(x, mod3, norm1_g, w_in, cos_t, sin_t)


def _attn_kernel(bias_ref, q_ref, k_ref, v_ref, o_ref, lse_ref, *, sub_len, tq, rr, pp):
    i = pl.program_id(3)
    lane_b = lax.broadcasted_iota(jnp.int32, (Q_TILE, PAIR), 1) < HEAD_DIM

    def one_residue(r):
        for sb in range(tq // Q_TILE):
            rows = slice(sb * Q_TILE, (sb + 1) * Q_TILE)
            m0 = i * tq + sb * Q_TILE
            start = jnp.clip(m0 - HALF_KEYS, 0, sub_len - K_WIN)
            start = pl.multiple_of(start, HALF_KEYS)
            bias = bias_ref[jnp.where(m0 == 0, 1, jnp.where(m0 == sub_len - Q_TILE, 2, 0))]
            for hp in range(pp):
                cols = slice(hp * PAIR, (hp + 1) * PAIR)
                q = q_ref[r, rows, cols]
                kw = k_ref[r, pl.ds(start, K_WIN), cols]
                vw = v_ref[r, pl.ds(start, K_WIN), cols]
                zero = jnp.zeros_like(q)
                q2 = jnp.concatenate([jnp.where(lane_b, q, zero), jnp.where(lane_b, zero, q)],
                                     axis=0)
                s = lax.dot_general(q2, kw, (((1,), (1,)), ((), ())),
                                    preferred_element_type=F32) + bias
                m = jnp.max(s, axis=1, keepdims=True)
                p = jnp.exp(s - m)
                l = jnp.sum(p, axis=1, keepdims=True)
                pv = jnp.dot(p.astype(BF16), vw, preferred_element_type=F32)
                o2 = pv / l
                lse2 = jnp.broadcast_to(m + jnp.log(l), (2 * Q_TILE, PAIR))
                o_ref[r, rows, cols] = jnp.where(lane_b, o2[:Q_TILE], o2[Q_TILE:])
                lse_ref[r, rows, cols] = jnp.where(lane_b, lse2[:Q_TILE], lse2[Q_TILE:])

    if rr == 1:
        one_residue(0)
    else:
        def body(r, carry):
            one_residue(r)
            return carry
        lax.fori_loop(0, rr, body, 0)


def _band_bias():
    row = lax.broadcasted_iota(jnp.int32, (2 * Q_TILE, K_WIN), 0) % Q_TILE
    col = lax.broadcasted_iota(jnp.int32, (2 * Q_TILE, K_WIN), 1)
    offs = jnp.array([-HALF_KEYS, 0, -2 * HALF_KEYS], jnp.int32)[:, None, None]
    valid = jnp.abs(col - row + offs) <= HALF_KEYS
    return jnp.where(valid, 0.0, NEG_BIG).astype(F32)


def _attn_call(qd, kd, vd, bias):
    b, d, sub_len, _ = qd.shape
    pp = max(1, min(N_PAIRS, (4 * 1024 * 1024) // (sub_len * PAIR * 2)))
    tq = min(512 if pp > 1 else 1024, sub_len)
    rr = max(1, min(d, 16 // (pp * (tq // Q_TILE))))
    qspec = pl.BlockSpec((None, rr, tq, pp * PAIR), lambda bi, r, hp, i: (bi, r, i, hp))
    kspec = pl.BlockSpec((None, rr, sub_len, pp * PAIR), lambda bi, r, hp, i: (bi, r, 0, hp))
    out_sd = jax.ShapeDtypeStruct((b, d, sub_len, ATTN_WIDTH), F32)
    return pl.pallas_call(
        functools.partial(_attn_kernel, sub_len=sub_len, tq=tq, rr=rr, pp=pp),
        grid=(b, d // rr, N_PAIRS // pp, sub_len // tq),
        in_specs=[_const_spec((3, 2 * Q_TILE, K_WIN)), qspec, kspec, kspec],
        out_specs=[qspec, qspec],
        out_shape=[out_sd, out_sd],
        compiler_params=_cparams(("arbitrary",) * 4),
        name=f"dilated_attn_d{d}",
    )(bias, qd, kd, vd)


def _ssm_in_kernel(u_ref, t_ref, we_ref, y_ref, e_ref, lhs_ref, *, mc):
    for j in range(CHUNK):
        lhs_ref[:, j * PAIR:(j + 1) * PAIR] = u_ref[pl.ds(j, mc, stride=CHUNK), :].astype(BF16)
    x = lhs_ref[...]
    y_ref[...] = jnp.dot(x, t_ref[...], preferred_element_type=F32)
    e_ref[...] = jnp.dot(x, we_ref[...], preferred_element_type=F32)


def _ssm_in_call(u, t_w, we_w, mc):
    b, s, _ = u.shape
    n_chunks = s // CHUNK
    tile = pl.BlockSpec((None, None, mc, BUNDLE_W), lambda bun, bi, i: (bi, bun, i, 0))
    wspec = pl.BlockSpec((None, BUNDLE_W, BUNDLE_W), lambda bun, bi, i: (bun, 0, 0),
                         pipeline_mode=pl.Buffered(1))
    out_sd = jax.ShapeDtypeStruct((b, N_BUNDLES, n_chunks, BUNDLE_W), F32)
    return pl.pallas_call(
        functools.partial(_ssm_in_kernel, mc=mc),
        grid=(N_BUNDLES, b, n_chunks // mc),
        in_specs=[pl.BlockSpec((None, CHUNK * mc, PAIR), lambda bun, bi, i: (bi, i, bun)),
                  wspec, wspec],
        out_specs=[tile, tile],
        out_shape=[out_sd, out_sd],
        scratch_shapes=[pltpu.VMEM((mc, BUNDLE_W), BF16)],
        compiler_params=_cparams(("arbitrary",) * 3),
        name="s5_chunk_in",
    )(u, t_w, we_w)


def _ssm_scan_kernel(e_ref, a_ref, h_ref, *, n_chunks):
    w = STATE_W
    coef = [(a_ref[dr, 0:1, :], a_ref[dr, 1:2, :]) for dr in range(2)]

    def body(s, carry):
        rows = (s, n_chunks - 1 - s)
        new = []
        for dr in range(2):
            ar, ai = coef[dr]
            hr, hi = carry[2 * dr], carry[2 * dr + 1]
            k = rows[dr]
            h_ref[pl.ds(k, 1), (2 * dr) * w:(2 * dr + 1) * w] = hr
            h_ref[pl.ds(k, 1), (2 * dr + 1) * w:(2 * dr + 2) * w] = hi
            er = e_ref[pl.ds(k, 1), (2 * dr) * w:(2 * dr + 1) * w]
            ei = e_ref[pl.ds(k, 1), (2 * dr + 1) * w:(2 * dr + 2) * w]
            new += [ar * hr - ai * hi + er, ar * hi + ai * hr + ei]
        return tuple(new)

    zero = jnp.zeros((1, w), F32)
    lax.fori_loop(0, n_chunks, body, (zero,) * 4, unroll=4)


def _ssm_scan_call(e, a_pow):
    b, _, n_chunks, _ = e.shape
    tile = pl.BlockSpec((None, None, n_chunks, 4 * STATE_W), lambda bi, bun: (bi, bun, 0, 0))
    return pl.pallas_call(
        functools.partial(_ssm_scan_kernel, n_chunks=n_chunks),
        grid=(b, N_BUNDLES),
        in_specs=[tile,
                  pl.BlockSpec((None, 2, 2, STATE_W), lambda bi, bun: (bun, 0, 0, 0))],
        out_specs=tile,
        out_shape=jax.ShapeDtypeStruct(e.shape, F32),
        compiler_params=_cparams(("arbitrary",) * 2),
        name="s5_chunk_scan",
    )(e, a_pow)


def _ssm_out_kernel(h_ref, v_ref, yi_ref, y_ref, *, mc):
    y = yi_ref[...] + jnp.dot(h_ref[...].astype(BF16), v_ref[...], preferred_element_type=F32)
    for t in range(CHUNK):
        y_ref[pl.ds(t, mc, stride=CHUNK), :] = y[:, t * PAIR:(t + 1) * PAIR]


def _ssm_out_call(h_in, v_w, y_intra, mc):
    b, _, n_chunks, _ = h_in.shape
    tile = pl.BlockSpec((None, None, mc, BUNDLE_W), lambda bun, bi, i: (bi, bun, i, 0))
    wspec = pl.BlockSpec((None, BUNDLE_W, BUNDLE_W), lambda bun, bi, i: (bun, 0, 0),
                         pipeline_mode=pl.Buffered(1))
    return pl.pallas_call(
        functools.partial(_ssm_out_kernel, mc=mc),
        grid=(N_BUNDLES, b, n_chunks // mc),
        in_specs=[tile, wspec, tile],
        out_specs=pl.BlockSpec((None, CHUNK * mc, PAIR), lambda bun, bi, i: (bi, i, bun)),
        out_shape=jax.ShapeDtypeStruct((b, n_chunks * CHUNK, SSM_WIDTH), F32),
        compiler_params=_cparams(("arbitrary",) * 3),
        name="s5_chunk_out",
    )(h_in, v_w, y_intra)


def _cpow(lam_re, lam_im, dt, n):
    nn = n.astype(F32).reshape(n.shape + (1, 1))
    mag = jnp.exp(nn * (lam_re * dt))
    ang = nn * (lam_im * dt)
    return mag * jnp.cos(ang), mag * jnp.sin(ang)


def _expand_kernel(c_ref, e_ref, o_ref, *, tr, row_div, col_div):
    w = jnp.dot(c_ref[...].astype(BF16), e_ref[...], preferred_element_type=F32)
    rows = lax.broadcasted_iota(jnp.int32, (tr, BUNDLE_W), 0) + pl.program_id(1) * tr
    cols = lax.broadcasted_iota(jnp.int32, (tr, BUNDLE_W), 1)
    same_group = (rows // row_div) % BUNDLE_GROUPS == (cols // col_div) % BUNDLE_GROUPS
    o_ref[...] = jnp.where(same_group, w, 0.0).astype(BF16)


def _expand_call(compact, spread, row_div, col_div, name):
    tr = 512
    return pl.pallas_call(
        functools.partial(_expand_kernel, tr=tr, row_div=row_div, col_div=col_div),
        grid=(N_BUNDLES, BUNDLE_W // tr),
        in_specs=[pl.BlockSpec((None, tr, 2 * PAIR), lambda bun, i: (bun, i, 0)),
                  pl.BlockSpec((2 * PAIR, BUNDLE_W), lambda bun, i: (0, 0))],
        out_specs=pl.BlockSpec((None, tr, BUNDLE_W), lambda bun, i: (bun, i, 0)),
        out_shape=jax.ShapeDtypeStruct((N_BUNDLES, BUNDLE_W, BUNDLE_W), BF16),
        compiler_params=_cparams(("arbitrary", "arbitrary")),
        name=name,
    )(compact, spread)


def _expand_readout_kernel(c_ref, e_ref, o_ref):
    rows = lax.broadcasted_iota(jnp.int32, (BUNDLE_W, PAIR), 0)
    cols = lax.broadcasted_iota(jnp.int32, (BUNDLE_W, PAIR), 1)
    same_group = (rows // SSM_STATE) % BUNDLE_GROUPS == cols // SSM_GROUP
    w = jnp.dot(c_ref[...].astype(BF16), e_ref[...], preferred_element_type=F32)
    o_ref[...] = jnp.where(same_group, w, 0.0).astype(BF16)


def _expand_readout_call(compact, spread):
    return pl.pallas_call(
        _expand_readout_kernel,
        grid=(N_BUNDLES,),
        in_specs=[pl.BlockSpec((None, BUNDLE_W, SSM_GROUP), lambda bun: (bun, 0, 0)),
                  pl.BlockSpec((SSM_GROUP, PAIR), lambda bun: (0, 0))],
        out_specs=pl.BlockSpec((None, BUNDLE_W, PAIR), lambda bun: (bun, 0, 0)),
        out_shape=jax.ShapeDtypeStruct((N_BUNDLES, BUNDLE_W, PAIR), BF16),
        compiler_params=_cparams(("arbitrary",)),
        name="s5_expand_readout",
    )(compact, spread)


def _toeplitz_kernel(we_ref, cm_ref, o_ref, lag_ref):
    half = 2 * STATE_W
    span = (CHUNK - 1) * PAIR
    gf = jnp.dot(we_ref[:, 0:half], cm_ref[0:half, :], preferred_element_type=F32)
    gb = jnp.dot(we_ref[:, half:], cm_ref[half:, :], preferred_element_type=F32)
    lag_ref[0:span, :] = gf[0:span]
    lag_ref[span:span + PAIR, :] = gf[span:] + gb[0:PAIR]
    lag_ref[span + PAIR:, :] = gb[PAIR:]
    for t in range(CHUNK):
        first = (CHUNK - 1 - t) * PAIR
        o_ref[:, t * PAIR:(t + 1) * PAIR] = lag_ref[first:first + BUNDLE_W, :].astype(BF16)


def _toeplitz_call(we_w, cm_w):
    return pl.pallas_call(
        _toeplitz_kernel,
        grid=(N_BUNDLES,),
        in_specs=[pl.BlockSpec((None, BUNDLE_W, BUNDLE_W), lambda bun: (bun, 0, 0)),
                  pl.BlockSpec((None, BUNDLE_W, PAIR), lambda bun: (bun, 0, 0))],
        out_specs=pl.BlockSpec((None, BUNDLE_W, BUNDLE_W), lambda bun: (bun, 0, 0)),
        out_shape=jax.ShapeDtypeStruct((N_BUNDLES, BUNDLE_W, BUNDLE_W), BF16),
        scratch_shapes=[pltpu.VMEM(((2 * CHUNK - 1) * PAIR, PAIR), F32)],
        compiler_params=_cparams(("arbitrary",)),
        name="s5_toeplitz",
    )(we_w, cm_w)


def _ssm_weights(lam_re, lam_im, log_dt, b_re, b_im, c_re, c_im):
    steps = jnp.arange(CHUNK)
    readout, w_e, v_c, a_pow = [], [], [], []
    for dr in range(2):
        lr, li = lam_re[dr], lam_im[dr]
        dt = jnp.exp(log_dt[dr])[:, None]
        a_r, a_i = _cpow(lr, li, dt, jnp.ones((), F32))
        nr = a_r - 1.0
        den = lr * lr + li * li
        z_r = ((nr * lr + a_i * li) / den)[..., None]
        z_i = ((a_i * lr - nr * li) / den)[..., None]
        bb_r = z_r * b_re[dr] - z_i * b_im[dr]
        bb_i = z_r * b_im[dr] + z_i * b_re[dr]
        cr = jnp.swapaxes(c_re[dr], 1, 2)
        ci = jnp.swapaxes(c_im[dr], 1, 2)

        readout.append((cr, -ci))

        e_r, e_i = _cpow(lr, li, dt, (CHUNK - 1 - steps) if dr == 0 else steps)
        we_r = e_r[..., None] * bb_r - e_i[..., None] * bb_i
        we_i = e_r[..., None] * bb_i + e_i[..., None] * bb_r
        w_e.append((we_r, we_i))

        o_r, o_i = _cpow(lr, li, dt, (steps + 1) if dr == 0 else (CHUNK - steps))
        vo_r = o_r[..., None] * cr - o_i[..., None] * ci
        vo_i = o_r[..., None] * ci + o_i[..., None] * cr
        v_c.append((vo_r, -vo_i))

        a_pow.append(_cpow(lr, li, dt, jnp.full((), CHUNK, F32)))

    def bundle(x):
        return x.reshape((x.shape[0], N_BUNDLES, BUNDLE_GROUPS) + x.shape[2:])

    cm = jnp.stack([jnp.stack(readout[dr], axis=0) for dr in range(2)], axis=0)
    cm = cm.reshape(2, 2, N_BUNDLES, BUNDLE_GROUPS, SSM_STATE, SSM_GROUP)
    cm_c = jnp.transpose(cm, (2, 0, 1, 3, 4, 5)).reshape(N_BUNDLES, BUNDLE_W, SSM_GROUP)

    quarters = [w_e[0][0], w_e[0][1], w_e[1][0], w_e[1][1]]
    we = jnp.stack([bundle(x) for x in quarters], axis=0)
    we_c = jnp.transpose(we, (2, 1, 3, 5, 0, 4)).reshape(N_BUNDLES, BUNDLE_W, 2 * PAIR)

    quarters = [v_c[0][0], v_c[0][1], v_c[1][0], v_c[1][1]]
    vv = jnp.stack([bundle(x) for x in quarters], axis=0)
    v_cmp = jnp.transpose(vv, (2, 0, 3, 4, 1, 5)).reshape(N_BUNDLES, BUNDLE_W, 2 * PAIR)

    src = jnp.arange(2 * PAIR)[:, None]
    dst = jnp.arange(BUNDLE_W)[None, :]
    spread_tc = ((src // SSM_GROUP == dst // PAIR) & (src % SSM_GROUP == dst % SSM_GROUP)).astype(BF16)
    spread_qp = ((src // SSM_STATE == dst // STATE_W) & (src % SSM_STATE == dst % SSM_STATE)).astype(BF16)

    spread_c = (jnp.arange(SSM_GROUP)[:, None] == jnp.arange(PAIR)[None, :] % SSM_GROUP).astype(BF16)
    we_w = _expand_call(we_c, spread_qp, SSM_GROUP, SSM_STATE, "s5_expand_inject")
    t_w = _toeplitz_call(we_w, _expand_readout_call(cm_c, spread_c))
    v_w = _expand_call(v_cmp, spread_tc, SSM_STATE, SSM_GROUP, "s5_expand_carry")

    ap = jnp.stack([jnp.stack([a_pow[dr][0], a_pow[dr][1]], axis=0) for dr in range(2)], axis=0)
    ap = ap.reshape(2, 2, N_BUNDLES, STATE_W)
    ap = jnp.transpose(ap, (2, 0, 1, 3))
    return t_w, we_w, v_w, ap


def _mix_kernel(x_ref, mod_ref, o1_ref, l1_ref, o4_ref, l4_ref, o16_ref, l16_ref,
                y_ref, u_ref, dskip_ref, wglu_ref, bglu_ref, ang_ref, sng_ref,
                wo_ref, x1_ref, nat_ref, tmp_ref, *, tm):
    n4, n16 = tm // 4, tm // 16

    def natural(ref, d, base):
        for c in range(N_PAIRS):
            cols = slice(c * PAIR, (c + 1) * PAIR)
            for r4 in range(4):
                if d == 16:
                    for rp in range(4):
                        tmp_ref[c, pl.ds(r4 * n4 + rp, n16, stride=4), :] = ref[r4 + 4 * rp, :, cols]
                    quarter = tmp_ref[c, r4 * n4:(r4 + 1) * n4, :]
                else:
                    quarter = ref[r4, :, cols]
                nat_ref[base + c, pl.ds(r4, n4, stride=4), :] = quarter
        return jnp.concatenate([nat_ref[base + c] for c in range(N_PAIRS)], axis=1)

    o2, l2 = natural(o4_ref, 4, 0), natural(l4_ref, 4, N_PAIRS)
    o3, l3 = natural(o16_ref, 16, 2 * N_PAIRS), natural(l16_ref, 16, 3 * N_PAIRS)
    l1 = l1_ref[...]
    lm = jnp.maximum(jnp.maximum(l1, l2), l3)
    e1, e2, e3 = jnp.exp(l1 - lm), jnp.exp(l2 - lm), jnp.exp(l3 - lm)
    attn = (e1 * o1_ref[...] + e2 * o2 + e3 * o3) / (e1 + e2 + e3)
    an = _rms(attn, ang_ref[...])

    y = y_ref[...] + dskip_ref[...] * u_ref[...]
    g = 0.5 * y * (1.0 + jnp.tanh(math.sqrt(2.0 / math.pi) * (y + 0.044715 * (y * y * y))))
    z = jnp.dot(g.astype(BF16), wglu_ref[...], preferred_element_type=F32) + bglu_ref[...]
    sn = _rms(g * _sigmoid(z), sng_ref[...])

    mixed = (jnp.dot(an.astype(BF16), wo_ref[0:ATTN_WIDTH, :], preferred_element_type=F32)
             + jnp.dot(sn.astype(BF16), wo_ref[ATTN_WIDTH:, :], preferred_element_type=F32))
    x1_ref[...] = x_ref[...] + mod_ref[2:3, :] * mixed


def _mix_call(x, mod3, branch_outs, y_ssm, u, d_skip, w_glu, b_glu, attn_g, ssm_g, w_o, tm):
    b, s, _ = x.shape
    row = lambda bi, i: (bi, i, 0)
    wide = pl.BlockSpec((None, tm, D_MODEL), row)
    half = pl.BlockSpec((None, tm, ATTN_WIDTH), row)
    res = lambda d: pl.BlockSpec((None, d, tm // d, ATTN_WIDTH), lambda bi, i: (bi, 0, i, 0))
    return pl.pallas_call(
        functools.partial(_mix_kernel, tm=tm),
        grid=(b, s // tm),
        in_specs=[wide, pl.BlockSpec((None, N_MOD, D_MODEL), lambda bi, i: (bi, 0, 0)),
                  half, half, res(4), res(4), res(16), res(16), half, half,
                  _const_spec((1, SSM_WIDTH)), _const_spec((SSM_WIDTH, SSM_WIDTH)),
                  _const_spec((1, SSM_WIDTH)), _const_spec((1, ATTN_WIDTH)),
                  _const_spec((1, SSM_WIDTH)), _const_spec((D_MODEL, D_MODEL))],
        out_specs=wide,
        out_shape=jax.ShapeDtypeStruct((b, s, D_MODEL), F32),
        scratch_shapes=[pltpu.VMEM((4 * N_PAIRS, tm, PAIR), F32),
                        pltpu.VMEM((N_PAIRS, tm, PAIR), F32)],
        compiler_params=_cparams(("arbitrary", "arbitrary")),
        name="merge_mix",
    )(x, mod3, *branch_outs, y_ssm, u, d_skip, w_glu, b_glu, attn_g, ssm_g, w_o)


FFN_CHUNK = 256


def _ffn_kernel(x1_ref, mod_ref, n2g_ref, w1_ref, w3_ref, w2_ref, fg_ref, out_ref, acc_ref):
    x1 = x1_ref[...]
    h = (_rms(x1, n2g_ref[...]) * (1.0 + mod_ref[4:5, :]) + mod_ref[3:4, :]).astype(BF16)
    for c in range(FFN_HIDDEN // FFN_CHUNK):
        cols = slice(c * FFN_CHUNK, (c + 1) * FFN_CHUNK)
        a = jnp.dot(h, w1_ref[:, cols], preferred_element_type=F32)
        bgate = jnp.dot(h, w3_ref[:, cols], preferred_element_type=F32)
        hid = (a * _sigmoid(a) * bgate).astype(BF16)
        part = jnp.dot(hid, w2_ref[cols, :], preferred_element_type=F32)
        if c == 0:
            acc_ref[...] = part
        else:
            acc_ref[...] += part
    x2 = x1 + mod_ref[5:6, :] * acc_ref[...]
    out_ref[...] = _rms(x2, fg_ref[...])


def _ffn_call(x1, mod3, norm2_g, w1, w3, w2, final_g, tm):
    b, s, _ = x1.shape
    wide = pl.BlockSpec((None, tm, D_MODEL), lambda bi, i: (bi, i, 0))
    return pl.pallas_call(
        _ffn_kernel,
        grid=(b, s // tm),
        in_specs=[wide, pl.BlockSpec((None, N_MOD, D_MODEL), lambda bi, i: (bi, 0, 0)),
                  _const_spec((1, D_MODEL)), _const_spec((D_MODEL, FFN_HIDDEN)),
                  _const_spec((D_MODEL, FFN_HIDDEN)), _const_spec((FFN_HIDDEN, D_MODEL)),
                  _const_spec((1, D_MODEL))],
        out_specs=wide,
        out_shape=jax.ShapeDtypeStruct((b, s, D_MODEL), F32),
        scratch_shapes=[pltpu.VMEM((tm, D_MODEL), F32)],
        compiler_params=_cparams(("arbitrary", "arbitrary")),
        name="swiglu_ffn",
    )(x1, mod3, norm2_g, w1, w3, w2, final_g)


def _rope_tables(seq_len):
    lane = jnp.arange(PAIR)
    inv = 1.0 / (ROPE_THETA ** ((2 * (lane % (HEAD_DIM // 2))).astype(F32) / HEAD_DIM))
    sign = jnp.where(lane % HEAD_DIM < HEAD_DIM // 2, -1.0, 1.0).astype(F32)
    ang = jnp.arange(seq_len, dtype=F32)[:, None] * inv[None, :]
    return jnp.cos(ang), jnp.sin(ang) * sign[None, :]


def _encode(x, mod3, p):
    b, s, _ = x.shape
    n_chunks = s // CHUNK
    cos_t, sin_t = _rope_tables(s)
    (q1, k1, v1, q4, k4, v4, q16, k16, v16, u) = _proj_call(
        x, mod3, p["norm1_g"], p["w_in"], cos_t, sin_t, tm=512)

    bias = _band_bias()
    o1, l1 = _attn_call(q1[:, None], k1[:, None], v1[:, None], bias)
    o4, l4 = _attn_call(q4, k4, v4, bias)
    o16, l16 = _attn_call(q16, k16, v16, bias)
    branch_outs = (o1[:, 0], l1[:, 0], o4, l4, o16, l16)

    mc = min(512, n_chunks)
    y_intra, e = _ssm_in_call(u, p["t_w"], p["we_w"], mc)
    h_in = _ssm_scan_call(e, p["a_pow"])
    y_ssm = _ssm_out_call(h_in, p["v_w"], y_intra, mc)

    x1 = _mix_call(x, mod3, branch_outs, y_ssm, u, p["d_skip"], p["w_glu"], p["b_glu"],
                   p["attn_norm_g"], p["ssm_norm_g"], p["w_o"], tm=512)
    return _ffn_call(x1, mod3, p["norm2_g"], p["w1"], p["w3"], p["w2"], p["final_g"], tm=1024)


def kernel(x_prompt, x_sample, c_prompt, c_sample, w_ada, b_ada, norm1_g, w_in, lam_re, lam_im,
           log_dt, b_re, b_im, c_re, c_im, d_skip, w_glu, b_glu, attn_norm_g, ssm_norm_g, w_o,
           norm2_g, w1, w3, w2, final_g):
    nb_p, nb_s = c_prompt.shape[0], c_sample.shape[0]
    rows = -(-(nb_p + nb_s) // 8) * 8
    c_all = jnp.concatenate([c_prompt, c_sample,
                             jnp.zeros((rows - nb_p - nb_s, D_MODEL), F32)], axis=0)
    mod = _mod_call(c_all, w_ada[0], b_ada[0][None, :]).reshape(rows, N_MOD, D_MODEL)

    t_w, we_w, v_w, a_pow = _ssm_weights(lam_re[0], lam_im[0], log_dt[0], b_re[0], b_im[0],
                                         c_re[0], c_im[0])
    p = dict(norm1_g=norm1_g[0][None, :], w_in=w_in[0].astype(BF16),
             t_w=t_w, we_w=we_w, v_w=v_w, a_pow=a_pow,
             d_skip=d_skip[0][None, :], w_glu=w_glu[0].astype(BF16), b_glu=b_glu[0][None, :],
             attn_norm_g=attn_norm_g[0][None, :], ssm_norm_g=ssm_norm_g[0][None, :],
             w_o=w_o[0].astype(BF16), norm2_g=norm2_g[0][None, :],
             w1=w1[0].astype(BF16), w3=w3[0].astype(BF16), w2=w2[0].astype(BF16),
             final_g=final_g[None, :])
    y_prompt = _encode(x_prompt, mod[:nb_p], p)
    y_sample = _encode(x_sample, mod[nb_p:nb_p + nb_s], p)
    return (y_prompt, y_sample)
```

```python
import functools
import math

import jax
import jax.numpy as jnp
from jax import lax
from jax.experimental import pallas as pl
from jax.experimental.pallas import tpu as pltpu

F32 = jnp.float32
BF16 = jnp.bfloat16

D_MODEL = 1024
ATTN_WIDTH = 512
SSM_WIDTH = 512
HEAD_DIM = 64
PAIR = 2 * HEAD_DIM
N_PAIRS = ATTN_WIDTH // PAIR
DILATIONS = (1, 4, 16)
HALF_KEYS = 64
Q_TILE = 128
K_WIN = Q_TILE + 2 * HALF_KEYS
ROPE_THETA = 10000.0
SSM_GROUP = 16
N_GROUPS = SSM_WIDTH // SSM_GROUP
SSM_STATE = 64
CHUNK = 16
BUNDLE_GROUPS = 8
N_BUNDLES = N_GROUPS // BUNDLE_GROUPS
BUNDLE_W = CHUNK * BUNDLE_GROUPS * SSM_GROUP
STATE_W = BUNDLE_GROUPS * SSM_STATE
FFN_HIDDEN = 2816
N_MOD = 6
EPS = 1e-6
NEG_BIG = -1e30
VMEM_LIMIT = 56 * 1024 * 1024


def _cparams(sem):
    return pltpu.CompilerParams(dimension_semantics=sem, vmem_limit_bytes=VMEM_LIMIT)


def _const_spec(shape):
    nd = len(shape)
    return pl.BlockSpec(shape, lambda *_: (0,) * nd, pipeline_mode=pl.Buffered(1))


def _sigmoid(x):
    return 1.0 / (1.0 + jnp.exp(-x))


def _rms(x, g):
    return x * lax.rsqrt(jnp.mean(x * x, axis=-1, keepdims=True) + EPS) * g


def _mod_kernel(c_ref, w_ref, b_ref, o_ref):
    c = c_ref[...]
    s = c * _sigmoid(c)
    o_ref[...] = jnp.dot(s.astype(BF16), w_ref[...].astype(BF16),
                         preferred_element_type=F32) + b_ref[...]


def _mod_call(c_all, w_ada, b_ada):
    rows = c_all.shape[0]
    n_out = w_ada.shape[1]
    tn = 1024
    return pl.pallas_call(
        _mod_kernel,
        grid=(n_out // tn,),
        in_specs=[pl.BlockSpec((rows, D_MODEL), lambda j: (0, 0)),
                  pl.BlockSpec((D_MODEL, tn), lambda j: (0, j)),
                  pl.BlockSpec((1, tn), lambda j: (0, j))],
        out_specs=pl.BlockSpec((rows, tn), lambda j: (0, j)),
        out_shape=jax.ShapeDtypeStruct((rows, n_out), F32),
        compiler_params=_cparams(("arbitrary",)),
        name="adaln_mod",
    )(c_all, w_ada, b_ada)


def _proj_kernel(x_ref, mod_ref, g_ref, w_ref, cos_ref, sin_ref,
                 q1_ref, k1_ref, v1_ref, q4_ref, k4_ref, v4_ref, q16_ref, k16_ref, v16_ref,
                 u_ref, res_ref, tmp_ref, *, tm):
    x = x_ref[...]
    h = _rms(x, g_ref[...]) * (1.0 + mod_ref[1:2, :]) + mod_ref[0:1, :]
    p = jnp.dot(h.astype(BF16), w_ref[...], preferred_element_type=F32)
    cos = jnp.concatenate([cos_ref[...]] * N_PAIRS, axis=1)
    sin = jnp.concatenate([sin_ref[...]] * N_PAIRS, axis=1)
    lane = lax.broadcasted_iota(jnp.int32, (1, ATTN_WIDTH), 1)
    first_half = (lane % HEAD_DIM) < (HEAD_DIM // 2)

    def rope(t):
        fwd = pltpu.roll(t, ATTN_WIDTH - HEAD_DIM // 2, 1)
        bwd = pltpu.roll(t, HEAD_DIM // 2, 1)
        return t * cos + jnp.where(first_half, fwd, bwd) * sin

    qkv = (rope(p[:, 0:ATTN_WIDTH]) * (HEAD_DIM ** -0.5),
           rope(p[:, ATTN_WIDTH:2 * ATTN_WIDTH]),
           p[:, 2 * ATTN_WIDTH:3 * ATTN_WIDTH])
    u_ref[...] = p[:, 3 * ATTN_WIDTH:]
    for a, (t, ref) in enumerate(zip(qkv, (q1_ref, k1_ref, v1_ref))):
        ref[...] = t.astype(BF16)
        for c in range(N_PAIRS):
            res_ref[a * N_PAIRS + c] = t[:, c * PAIR:(c + 1) * PAIR]
    n4, n16 = tm // 4, tm // 16
    for a, (ref4, ref16) in enumerate(((q4_ref, q16_ref), (k4_ref, k16_ref), (v4_ref, v16_ref))):
        for c in range(N_PAIRS):
            slab = a * N_PAIRS + c
            cols = slice(c * PAIR, (c + 1) * PAIR)
            for r4 in range(4):
                t4 = res_ref[slab, pl.ds(r4, n4, stride=4), :]
                ref4[r4, :, cols] = t4.astype(BF16)
                tmp_ref[slab, r4 * n4:(r4 + 1) * n4, :] = t4
            for r4 in range(4):
                for rp in range(4):
                    piece = tmp_ref[slab, pl.ds(r4 * n4 + rp, n16, stride=4), :]
                    ref16[r4 + 4 * rp, :, cols] = piece.astype(BF16)


def _proj_call(x, mod3, norm1_g, w_in, cos_t, sin_t, tm):
    b, s, _ = x.shape
    row = lambda bi, i: (bi, i, 0)
    nat = pl.BlockSpec((None, tm, ATTN_WIDTH), row)
    out_specs, out_shape = [nat] * 3, [jax.ShapeDtypeStruct((b, s, ATTN_WIDTH), BF16)] * 3
    for d in DILATIONS[1:]:
        out_specs += [pl.BlockSpec((None, d, tm // d, ATTN_WIDTH), lambda bi, i: (bi, 0, i, 0))] * 3
        out_shape += [jax.ShapeDtypeStruct((b, d, s // d, ATTN_WIDTH), BF16)] * 3
    out_specs.append(pl.BlockSpec((None, tm, SSM_WIDTH), row))
    out_shape.append(jax.ShapeDtypeStruct((b, s, SSM_WIDTH), F32))
    return pl.pallas_call(
        functools.partial(_proj_kernel, tm=tm),
        grid=(b, s // tm),
        in_specs=[pl.BlockSpec((None, tm, D_MODEL), row),
                  pl.BlockSpec((None, N_MOD, D_MODEL), lambda bi, i: (bi, 0, 0)),
                  _const_spec((1, D_MODEL)),
                  _const_spec((D_MODEL, 4 * ATTN_WIDTH)),
                  pl.BlockSpec((tm, PAIR), lambda bi, i: (i, 0)),
                  pl.BlockSpec((tm, PAIR), lambda bi, i: (i, 0))],
        out_specs=out_specs,
        out_shape=out_shape,
        scratch_shapes=[pltpu.VMEM((3 * N_PAIRS, tm, PAIR), F32)] * 2,
        compiler_params=_cparams(("arbitrary", "arbitrary")),
        name="proj_rope",
    )(x, mod3, norm1_g, w_in, cos_t, sin_t)


def _attn_kernel(bias_ref, q_ref, k_ref, v_ref, o_ref, lse_ref, *, sub_len, tq, rr, pp):
    i = pl.program_id(3)
    lane_b = lax.broadcasted_iota(jnp.int32, (Q_TILE, PAIR), 1) < HEAD_DIM

    def one_residue(r):
        for sb in range(tq // Q_TILE):
            rows = slice(sb * Q_TILE, (sb + 1) * Q_TILE)
            m0 = i * tq + sb * Q_TILE
            start = jnp.clip(m0 - HALF_KEYS, 0, sub_len - K_WIN)
            start = pl.multiple_of(start, HALF_KEYS)
            bias = bias_ref[jnp.where(m0 == 0, 1, jnp.where(m0 == sub_len - Q_TILE, 2, 0))]
            for hp in range(pp):
                cols = slice(hp * PAIR, (hp + 1) * PAIR)
                q = q_ref[r, rows, cols]
                kw = k_ref[r, pl.ds(start, K_WIN), cols]
                vw = v_ref[r, pl.ds(start, K_WIN), cols]
                zero = jnp.zeros_like(q)
                q2 = jnp.concatenate([jnp.where(lane_b, q, zero), jnp.where(lane_b, zero, q)],
                                     axis=0)
                s = lax.dot_general(q2, kw, (((1,), (1,)), ((), ())),
                                    preferred_element_type=F32) + bias
                m = jnp.max(s, axis=1, keepdims=True)
                p = jnp.exp(s - m)
                l = jnp.sum(p, axis=1, keepdims=True)
                pv = jnp.dot(p.astype(BF16), vw, preferred_element_type=F32)
                o2 = pv / l
                lse2 = jnp.broadcast_to(m + jnp.log(l), (2 * Q_TILE, PAIR))
                o_ref[r, rows, cols] = jnp.where(lane_b, o2[:Q_TILE], o2[Q_TILE:]).astype(BF16)
                lse_ref[r, rows, cols] = jnp.where(lane_b, lse2[:Q_TILE], lse2[Q_TILE:])

    if rr == 1:
        one_residue(0)
    else:
        def body(r, carry):
            one_residue(r)
            return carry
        lax.fori_loop(0, rr, body, 0)


def _band_bias():
    row = lax.broadcasted_iota(jnp.int32, (2 * Q_TILE, K_WIN), 0) % Q_TILE
    col = lax.broadcasted_iota(jnp.int32, (2 * Q_TILE, K_WIN), 1)
    offs = jnp.array([-HALF_KEYS, 0, -2 * HALF_KEYS], jnp.int32)[:, None, None]
    valid = jnp.abs(col - row + offs) <= HALF_KEYS
    return jnp.where(valid, 0.0, NEG_BIG).astype(F32)


def _attn_call(qd, kd, vd, bias):
    b, d, sub_len, _ = qd.shape
    pp = max(1, min(N_PAIRS, (4 * 1024 * 1024) // (sub_len * PAIR * 2)))
    tq = min(512 if pp > 1 else 1024, sub_len)
    rr = max(1, min(d, 16 // (pp * (tq // Q_TILE))))
    qspec = pl.BlockSpec((None, rr, tq, pp * PAIR), lambda bi, r, hp, i: (bi, r, i, hp))
    kspec = pl.BlockSpec((None, rr, sub_len, pp * PAIR), lambda bi, r, hp, i: (bi, r, 0, hp))
    out_sd = lambda dt: jax.ShapeDtypeStruct((b, d, sub_len, ATTN_WIDTH), dt)
    return pl.pallas_call(
        functools.partial(_attn_kernel, sub_len=sub_len, tq=tq, rr=rr, pp=pp),
        grid=(b, d // rr, N_PAIRS // pp, sub_len // tq),
        in_specs=[_const_spec((3, 2 * Q_TILE, K_WIN)), qspec, kspec, kspec],
        out_specs=[qspec, qspec],
        out_shape=[out_sd(BF16), out_sd(F32)],
        compiler_params=_cparams(("arbitrary",) * 4),
        name=f"dilated_attn_d{d}",
    )(bias, qd, kd, vd)


def _ssm_in_kernel(u_ref, t_ref, we_ref, y_ref, e_ref, lhs_ref, *, mc):
    for j in range(CHUNK):
        lhs_ref[:, j * PAIR:(j + 1) * PAIR] = u_ref[pl.ds(j, mc, stride=CHUNK), :].astype(BF16)
    x = lhs_ref[...]
    y_ref[...] = jnp.dot(x, t_ref[...], preferred_element_type=F32)
    e_ref[...] = jnp.dot(x, we_ref[...], preferred_element_type=F32)


def _ssm_in_call(u, t_w, we_w, mc):
    b, s, _ = u.shape
    n_chunks = s // CHUNK
    tile = pl.BlockSpec((None, None, mc, BUNDLE_W), lambda bun, bi, i: (bi, bun, i, 0))
    wspec = pl.BlockSpec((None, BUNDLE_W, BUNDLE_W), lambda bun, bi, i: (bun, 0, 0),
                         pipeline_mode=pl.Buffered(1))
    out_sd = jax.ShapeDtypeStruct((b, N_BUNDLES, n_chunks, BUNDLE_W), F32)
    return pl.pallas_call(
        functools.partial(_ssm_in_kernel, mc=mc),
        grid=(N_BUNDLES, b, n_chunks // mc),
        in_specs=[pl.BlockSpec((None, CHUNK * mc, PAIR), lambda bun, bi, i: (bi, i, bun)),
                  wspec, wspec],
        out_specs=[tile, tile],
        out_shape=[out_sd, out_sd],
        scratch_shapes=[pltpu.VMEM((mc, BUNDLE_W), BF16)],
        compiler_params=_cparams(("arbitrary",) * 3),
        name="s5_chunk_in",
    )(u, t_w, we_w)


def _ssm_scan_kernel(e_ref, a_ref, h_ref, *, n_chunks):
    w = STATE_W
    coef = [(a_ref[dr, 0:1, :], a_ref[dr, 1:2, :]) for dr in range(2)]

    def body(s, carry):
        rows = (s, n_chunks - 1 - s)
        new = []
        for dr in range(2):
            ar, ai = coef[dr]
            hr, hi = carry[2 * dr], carry[2 * dr + 1]
            k = rows[dr]
            h_ref[pl.ds(k, 1), (2 * dr) * w:(2 * dr + 1) * w] = hr
            h_ref[pl.ds(k, 1), (2 * dr + 1) * w:(2 * dr + 2) * w] = hi
            er = e_ref[pl.ds(k, 1), (2 * dr) * w:(2 * dr + 1) * w]
            ei = e_ref[pl.ds(k, 1), (2 * dr + 1) * w:(2 * dr + 2) * w]
            new += [ar * hr - ai * hi + er, ar * hi + ai * hr + ei]
        return tuple(new)

    zero = jnp.zeros((1, w), F32)
    lax.fori_loop(0, n_chunks, body, (zero,) * 4, unroll=4)


def _ssm_scan_call(e, a_pow):
    b, _, n_chunks, _ = e.shape
    tile = pl.BlockSpec((None, None, n_chunks, 4 * STATE_W), lambda bi, bun: (bi, bun, 0, 0))
    return pl.pallas_call(
        functools.partial(_ssm_scan_kernel, n_chunks=n_chunks),
        grid=(b, N_BUNDLES),
        in_specs=[tile,
                  pl.BlockSpec((None, 2, 2, STATE_W), lambda bi, bun: (bun, 0, 0, 0))],
        out_specs=tile,
        out_shape=jax.ShapeDtypeStruct(e.shape, F32),
        compiler_params=_cparams(("arbitrary",) * 2),
        name="s5_chunk_scan",
    )(e, a_pow)


def _ssm_out_kernel(h_ref, v_ref, yi_ref, y_ref, *, mc):
    y = yi_ref[...] + jnp.dot(h_ref[...].astype(BF16), v_ref[...], preferred_element_type=F32)
    for t in range(CHUNK):
        y_ref[pl.ds(t, mc, stride=CHUNK), :] = y[:, t * PAIR:(t + 1) * PAIR]


def _ssm_out_call(h_in, v_w, y_intra, mc):
    b, _, n_chunks, _ = h_in.shape
    tile = pl.BlockSpec((None, None, mc, BUNDLE_W), lambda bun, bi, i: (bi, bun, i, 0))
    wspec = pl.BlockSpec((None, BUNDLE_W, BUNDLE_W), lambda bun, bi, i: (bun, 0, 0),
                         pipeline_mode=pl.Buffered(1))
    return pl.pallas_call(
        functools.partial(_ssm_out_kernel, mc=mc),
        grid=(N_BUNDLES, b, n_chunks // mc),
        in_specs=[tile, wspec, tile],
        out_specs=pl.BlockSpec((None, CHUNK * mc, PAIR), lambda bun, bi, i: (bi, i, bun)),
        out_shape=jax.ShapeDtypeStruct((b, n_chunks * CHUNK, SSM_WIDTH), F32),
        compiler_params=_cparams(("arbitrary",) * 3),
        name="s5_chunk_out",
    )(h_in, v_w, y_intra)


def _cpow(lam_re, lam_im, dt, n):
    nn = n.astype(F32).reshape(n.shape + (1, 1))
    mag = jnp.exp(nn * (lam_re * dt))
    ang = nn * (lam_im * dt)
    return mag * jnp.cos(ang), mag * jnp.sin(ang)


def _expand_kernel(c_ref, e_ref, o_ref, *, tr, row_div, col_div):
    w = jnp.dot(c_ref[...].astype(BF16), e_ref[...], preferred_element_type=F32)
    rows = lax.broadcasted_iota(jnp.int32, (tr, BUNDLE_W), 0) + pl.program_id(1) * tr
    cols = lax.broadcasted_iota(jnp.int32, (tr, BUNDLE_W), 1)
    same_group = (rows // row_div) % BUNDLE_GROUPS == (cols // col_div) % BUNDLE_GROUPS
    o_ref[...] = jnp.where(same_group, w, 0.0).astype(BF16)


def _expand_call(compact, spread, row_div, col_div, name):
    tr = 512
    return pl.pallas_call(
        functools.partial(_expand_kernel, tr=tr, row_div=row_div, col_div=col_div),
        grid=(N_BUNDLES, BUNDLE_W // tr),
        in_specs=[pl.BlockSpec((None, tr, 2 * PAIR), lambda bun, i: (bun, i, 0)),
                  pl.BlockSpec((2 * PAIR, BUNDLE_W), lambda bun, i: (0, 0))],
        out_specs=pl.BlockSpec((None, tr, BUNDLE_W), lambda bun, i: (bun, i, 0)),
        out_shape=jax.ShapeDtypeStruct((N_BUNDLES, BUNDLE_W, BUNDLE_W), BF16),
        compiler_params=_cparams(("arbitrary", "arbitrary")),
        name=name,
    )(compact, spread)


def _expand_readout_kernel(c_ref, e_ref, o_ref):
    rows = lax.broadcasted_iota(jnp.int32, (BUNDLE_W, PAIR), 0)
    cols = lax.broadcasted_iota(jnp.int32, (BUNDLE_W, PAIR), 1)
    same_group = (rows // SSM_STATE) % BUNDLE_GROUPS == cols // SSM_GROUP
    w = jnp.dot(c_ref[...].astype(BF16), e_ref[...], preferred_element_type=F32)
    o_ref[...] = jnp.where(same_group, w, 0.0).astype(BF16)


def _expand_readout_call(compact, spread):
    return pl.pallas_call(
        _expand_readout_kernel,
        grid=(N_BUNDLES,),
        in_specs=[pl.BlockSpec((None, BUNDLE_W, SSM_GROUP), lambda bun: (bun, 0, 0)),
                  pl.BlockSpec((SSM_GROUP, PAIR), lambda bun: (0, 0))],
        out_specs=pl.BlockSpec((None, BUNDLE_W, PAIR), lambda bun: (bun, 0, 0)),
        out_shape=jax.ShapeDtypeStruct((N_BUNDLES, BUNDLE_W, PAIR), BF16),
        compiler_params=_cparams(("arbitrary",)),
        name="s5_expand_readout",
    )(compact, spread)


def _toeplitz_kernel(we_ref, cm_ref, o_ref, lag_ref):
    half = 2 * STATE_W
    span = (CHUNK - 1) * PAIR
    gf = jnp.dot(we_ref[:, 0:half], cm_ref[0:half, :], preferred_element_type=F32)
    gb = jnp.dot(we_ref[:, half:], cm_ref[half:, :], preferred_element_type=F32)
    lag_ref[0:span, :] = gf[0:span]
    lag_ref[span:span + PAIR, :] = gf[span:] + gb[0:PAIR]
    lag_ref[span + PAIR:, :] = gb[PAIR:]
    for t in range(CHUNK):
        first = (CHUNK - 1 - t) * PAIR
        o_ref[:, t * PAIR:(t + 1) * PAIR] = lag_ref[first:first + BUNDLE_W, :].astype(BF16)


def _toeplitz_call(we_w, cm_w):
    return pl.pallas_call(
        _toeplitz_kernel,
        grid=(N_BUNDLES,),
        in_specs=[pl.BlockSpec((None, BUNDLE_W, BUNDLE_W), lambda bun: (bun, 0, 0)),
                  pl.BlockSpec((None, BUNDLE_W, PAIR), lambda bun: (bun, 0, 0))],
        out_specs=pl.BlockSpec((None, BUNDLE_W, BUNDLE_W), lambda bun: (bun, 0, 0)),
        out_shape=jax.ShapeDtypeStruct((N_BUNDLES, BUNDLE_W, BUNDLE_W), BF16),
        scratch_shapes=[pltpu.VMEM(((2 * CHUNK - 1) * PAIR, PAIR), F32)],
        compiler_params=_cparams(("arbitrary",)),
        name="s5_toeplitz",
    )(we_w, cm_w)


def _ssm_weights(lam_re, lam_im, log_dt, b_re, b_im, c_re, c_im):
    steps = jnp.arange(CHUNK)
    readout, w_e, v_c, a_pow = [], [], [], []
    for dr in range(2):
        lr, li = lam_re[dr], lam_im[dr]
        dt = jnp.exp(log_dt[dr])[:, None]
        a_r, a_i = _cpow(lr, li, dt, jnp.ones((), F32))
        nr = a_r - 1.0
        den = lr * lr + li * li
        z_r = ((nr * lr + a_i * li) / den)[..., None]
        z_i = ((a_i * lr - nr * li) / den)[..., None]
        bb_r = z_r * b_re[dr] - z_i * b_im[dr]
        bb_i = z_r * b_im[dr] + z_i * b_re[dr]
        cr = jnp.swapaxes(c_re[dr], 1, 2)
        ci = jnp.swapaxes(c_im[dr], 1, 2)

        readout.append((cr, -ci))

        e_r, e_i = _cpow(lr, li, dt, (CHUNK - 1 - steps) if dr == 0 else steps)
        we_r = e_r[..., None] * bb_r - e_i[..., None] * bb_i
        we_i = e_r[..., None] * bb_i + e_i[..., None] * bb_r
        w_e.append((we_r, we_i))

        o_r, o_i = _cpow(lr, li, dt, (steps + 1) if dr == 0 else (CHUNK - steps))
        vo_r = o_r[..., None] * cr - o_i[..., None] * ci
        vo_i = o_r[..., None] * ci + o_i[..., None] * cr
        v_c.append((vo_r, -vo_i))

        a_pow.append(_cpow(lr, li, dt, jnp.full((), CHUNK, F32)))

    def bundle(x):
        return x.reshape((x.shape[0], N_BUNDLES, BUNDLE_GROUPS) + x.shape[2:])

    cm = jnp.stack([jnp.stack(readout[dr], axis=0) for dr in range(2)], axis=0)
    cm = cm.reshape(2, 2, N_BUNDLES, BUNDLE_GROUPS, SSM_STATE, SSM_GROUP)
    cm_c = jnp.transpose(cm, (2, 0, 1, 3, 4, 5)).reshape(N_BUNDLES, BUNDLE_W, SSM_GROUP)

    quarters = [w_e[0][0], w_e[0][1], w_e[1][0], w_e[1][1]]
    we = jnp.stack([bundle(x) for x in quarters], axis=0)
    we_c = jnp.transpose(we, (2, 1, 3, 5, 0, 4)).reshape(N_BUNDLES, BUNDLE_W, 2 * PAIR)

    quarters = [v_c[0][0], v_c[0][1], v_c[1][0], v_c[1][1]]
    vv = jnp.stack([bundle(x) for x in quarters], axis=0)
    v_cmp = jnp.transpose(vv, (2, 0, 3, 4, 1, 5)).reshape(N_BUNDLES, BUNDLE_W, 2 * PAIR)

    src = jnp.arange(2 * PAIR)[:, None]
    dst = jnp.arange(BUNDLE_W)[None, :]
    spread_tc = ((src // SSM_GROUP == dst // PAIR) & (src % SSM_GROUP == dst % SSM_GROUP)).astype(BF16)
    spread_qp = ((src // SSM_STATE == dst // STATE_W) & (src % SSM_STATE == dst % SSM_STATE)).astype(BF16)

    spread_c = (jnp.arange(SSM_GROUP)[:, None] == jnp.arange(PAIR)[None, :] % SSM_GROUP).astype(BF16)
    we_w = _expand_call(we_c, spread_qp, SSM_GROUP, SSM_STATE, "s5_expand_inject")
    t_w = _toeplitz_call(we_w, _expand_readout_call(cm_c, spread_c))
    v_w = _expand_call(v_cmp, spread_tc, SSM_STATE, SSM_GROUP, "s5_expand_carry")

    ap = jnp.stack([jnp.stack([a_pow[dr][0], a_pow[dr][1]], axis=0) for dr in range(2)], axis=0)
    ap = ap.reshape(2, 2, N_BUNDLES, STATE_W)
    ap = jnp.transpose(ap, (2, 0, 1, 3))
    return t_w, we_w, v_w, ap


def _mix_kernel(x_ref, mod_ref, o1_ref, l1_ref, o4_ref, l4_ref, o16_ref, l16_ref,
                y_ref, u_ref, dskip_ref, wglu_ref, bglu_ref, ang_ref, sng_ref,
                wo_ref, x1_ref, nat_ref, tmp_ref, *, tm):
    n4, n16 = tm // 4, tm // 16

    def natural(ref, d, base):
        for c in range(N_PAIRS):
            cols = slice(c * PAIR, (c + 1) * PAIR)
            for r4 in range(4):
                if d == 16:
                    for rp in range(4):
                        tmp_ref[c, pl.ds(r4 * n4 + rp, n16, stride=4), :] = (
                            ref[r4 + 4 * rp, :, cols].astype(F32))
                    quarter = tmp_ref[c, r4 * n4:(r4 + 1) * n4, :]
                else:
                    quarter = ref[r4, :, cols].astype(F32)
                nat_ref[base + c, pl.ds(r4, n4, stride=4), :] = quarter
        return jnp.concatenate([nat_ref[base + c] for c in range(N_PAIRS)], axis=1)

    o2, l2 = natural(o4_ref, 4, 0), natural(l4_ref, 4, N_PAIRS)
    o3, l3 = natural(o16_ref, 16, 2 * N_PAIRS), natural(l16_ref, 16, 3 * N_PAIRS)
    l1 = l1_ref[...]
    lm = jnp.maximum(jnp.maximum(l1, l2), l3)
    e1, e2, e3 = jnp.exp(l1 - lm), jnp.exp(l2 - lm), jnp.exp(l3 - lm)
    attn = (e1 * o1_ref[...].astype(F32) + e2 * o2 + e3 * o3) / (e1 + e2 + e3)
    an = _rms(attn, ang_ref[...])

    y = y_ref[...] + dskip_ref[...] * u_ref[...]
    g = 0.5 * y * (1.0 + jnp.tanh(math.sqrt(2.0 / math.pi) * (y + 0.044715 * (y * y * y))))
    z = jnp.dot(g.astype(BF16), wglu_ref[...], preferred_element_type=F32) + bglu_ref[...]
    sn = _rms(g * _sigmoid(z), sng_ref[...])

    mixed = (jnp.dot(an.astype(BF16), wo_ref[0:ATTN_WIDTH, :], preferred_element_type=F32)
             + jnp.dot(sn.astype(BF16), wo_ref[ATTN_WIDTH:, :], preferred_element_type=F32))
    x1_ref[...] = x_ref[...] + mod_ref[2:3, :] * mixed


def _mix_call(x, mod3, branch_outs, y_ssm, u, d_skip, w_glu, b_glu, attn_g, ssm_g, w_o, tm):
    b, s, _ = x.shape
    row = lambda bi, i: (bi, i, 0)
    wide = pl.BlockSpec((None, tm, D_MODEL), row)
    half = pl.BlockSpec((None, tm, ATTN_WIDTH), row)
    res = lambda d: pl.BlockSpec((None, d, tm // d, ATTN_WIDTH), lambda bi, i: (bi, 0, i, 0))
    return pl.pallas_call(
        functools.partial(_mix_kernel, tm=tm),
        grid=(b, s // tm),
        in_specs=[wide, pl.BlockSpec((None, N_MOD, D_MODEL), lambda bi, i: (bi, 0, 0)),
                  half, half, res(4), res(4), res(16), res(16), half, half,
                  _const_spec((1, SSM_WIDTH)), _const_spec((SSM_WIDTH, SSM_WIDTH)),
                  _const_spec((1, SSM_WIDTH)), _const_spec((1, ATTN_WIDTH)),
                  _const_spec((1, SSM_WIDTH)), _const_spec((D_MODEL, D_MODEL))],
        out_specs=wide,
        out_shape=jax.ShapeDtypeStruct((b, s, D_MODEL), F32),
        scratch_shapes=[pltpu.VMEM((4 * N_PAIRS, tm, PAIR), F32),
                        pltpu.VMEM((N_PAIRS, tm, PAIR), F32)],
        compiler_params=_cparams(("arbitrary", "arbitrary")),
        name="merge_mix",
    )(x, mod3, *branch_outs, y_ssm, u, d_skip, w_glu, b_glu, attn_g, ssm_g, w_o)


FFN_CHUNK = 256


def _ffn_kernel(x1_ref, mod_ref, n2g_ref, w1_ref, w3_ref, w2_ref, fg_ref, out_ref, acc_ref):
    x1 = x1_ref[...]
    h = (_rms(x1, n2g_ref[...]) * (1.0 + mod_ref[4:5, :]) + mod_ref[3:4, :]).astype(BF16)
    for c in range(FFN_HIDDEN // FFN_CHUNK):
        cols = slice(c * FFN_CHUNK, (c + 1) * FFN_CHUNK)
        a = jnp.dot(h, w1_ref[:, cols], preferred_element_type=F32)
        bgate = jnp.dot(h, w3_ref[:, cols], preferred_element_type=F32)
        hid = (a * _sigmoid(a) * bgate).astype(BF16)
        part = jnp.dot(hid, w2_ref[cols, :], preferred_element_type=F32)
        if c == 0:
            acc_ref[...] = part
        else:
            acc_ref[...] += part
    x2 = x1 + mod_ref[5:6, :] * acc_ref[...]
    out_ref[...] = _rms(x2, fg_ref[...])


def _ffn_call(x1, mod3, norm2_g, w1, w3, w2, final_g, tm):
    b, s, _ = x1.shape
    wide = pl.BlockSpec((None, tm, D_MODEL), lambda bi, i: (bi, i, 0))
    return pl.pallas_call(
        _ffn_kernel,
        grid=(b, s // tm),
        in_specs=[wide, pl.BlockSpec((None, N_MOD, D_MODEL), lambda bi, i: (bi, 0, 0)),
                  _const_spec((1, D_MODEL)), _const_spec((D_MODEL, FFN_HIDDEN)),
                  _const_spec((D_MODEL, FFN_HIDDEN)), _const_spec((FFN_HIDDEN, D_MODEL)),
                  _const_spec((1, D_MODEL))],
        out_specs=wide,
        out_shape=jax.ShapeDtypeStruct((b, s, D_MODEL), F32),
        scratch_shapes=[pltpu.VMEM((tm, D_MODEL), F32)],
        compiler_params=_cparams(("arbitrary", "arbitrary")),
        name="swiglu_ffn",
    )(x1, mod3, norm2_g, w1, w3, w2, final_g)


def _rope_tables(seq_len):
    lane = jnp.arange(PAIR)
    inv = 1.0 / (ROPE_THETA ** ((2 * (lane % (HEAD_DIM // 2))).astype(F32) / HEAD_DIM))
    sign = jnp.where(lane % HEAD_DIM < HEAD_DIM // 2, -1.0, 1.0).astype(F32)
    ang = jnp.arange(seq_len, dtype=F32)[:, None] * inv[None, :]
    return jnp.cos(ang), jnp.sin(ang) * sign[None, :]


def _encode(x, mod3, p):
    b, s, _ = x.shape
    n_chunks = s // CHUNK
    cos_t, sin_t = _rope_tables(s)
    (q1, k1, v1, q4, k4, v4, q16, k16, v16, u) = _proj_call(
        x, mod3, p["norm1_g"], p["w_in"], cos_t, sin_t, tm=512)

    bias = _band_bias()
    o1, l1 = _attn_call(q1[:, None], k1[:, None], v1[:, None], bias)
    o4, l4 = _attn_call(q4, k4, v4, bias)
    o16, l16 = _attn_call(q16, k16, v16, bias)
    branch_outs = (o1[:, 0], l1[:, 0], o4, l4, o16, l16)

    mc = min(512, n_chunks)
    y_intra, e = _ssm_in_call(u, p["t_w"], p["we_w"], mc)
    h_in = _ssm_scan_call(e, p["a_pow"])
    y_ssm = _ssm_out_call(h_in, p["v_w"], y_intra, mc)

    x1 = _mix_call(x, mod3, branch_outs, y_ssm, u, p["d_skip"], p["w_glu"], p["b_glu"],
                   p["attn_norm_g"], p["ssm_norm_g"], p["w_o"], tm=512)
    return _ffn_call(x1, mod3, p["norm2_g"], p["w1"], p["w3"], p["w2"], p["final_g"], tm=1024)


def kernel(x_prompt, x_sample, c_prompt, c_sample, w_ada, b_ada, norm1_g, w_in, lam_re, lam_im,
           log_dt, b_re, b_im, c_re, c_im, d_skip, w_glu, b_glu, attn_norm_g, ssm_norm_g, w_o,
           norm2_g, w1, w3, w2, final_g):
    nb_p, nb_s = c_prompt.shape[0], c_sample.shape[0]
    rows = -(-(nb_p + nb_s) // 8) * 8
    c_all = jnp.concatenate([c_prompt, c_sample,
                             jnp.zeros((rows - nb_p - nb_s, D_MODEL), F32)], axis=0)
    mod = _mod_call(c_all, w_ada[0], b_ada[0][None, :]).reshape(rows, N_MOD, D_MODEL)

    t_w, we_w, v_w, a_pow = _ssm_weights(lam_re[0], lam_im[0], log_dt[0], b_re[0], b_im[0],
                                         c_re[0], c_im[0])
    p = dict(norm1_g=norm1_g[0][None, :], w_in=w_in[0].astype(BF16),
             t_w=t_w, we_w=we_w, v_w=v_w, a_pow=a_pow,
             d_skip=d_skip[0][None, :], w_glu=w_glu[0].astype(BF16), b_glu=b_glu[0][None, :],
             attn_norm_g=attn_norm_g[0][None, :], ssm_norm_g=ssm_norm_g[0][None, :],
             w_o=w_o[0].astype(BF16), norm2_g=norm2_g[0][None, :],
             w1=w1[0].astype(BF16), w3=w3[0].astype(BF16), w2=w2[0].astype(BF16),
             final_g=final_g[None, :])
    y_prompt = _encode(x_prompt, mod[:nb_p], p)
    y_sample = _encode(x_sample, mod[nb_p:nb_p + nb_s], p)
    return (y_prompt, y_sample)
```

```python
import functools
import math

import jax
import jax.numpy as jnp
from jax import lax
from jax.experimental import pallas as pl
from jax.experimental.pallas import tpu as pltpu

F32 = jnp.float32
BF16 = jnp.bfloat16

D_MODEL = 1024
ATTN_WIDTH = 512
SSM_WIDTH = 512
HEAD_DIM = 64
PAIR = 2 * HEAD_DIM
N_PAIRS = ATTN_WIDTH // PAIR
DILATIONS = (1, 4, 16)
HALF_KEYS = 64
Q_TILE = 128
K_WIN = Q_TILE + 2 * HALF_KEYS
ROPE_THETA = 10000.0
SSM_GROUP = 16
N_GROUPS = SSM_WIDTH // SSM_GROUP
SSM_STATE = 64
CHUNK = 16
BUNDLE_GROUPS = 8
N_BUNDLES = N_GROUPS // BUNDLE_GROUPS
BUNDLE_W = CHUNK * BUNDLE_GROUPS * SSM_GROUP
STATE_W = BUNDLE_GROUPS * SSM_STATE
FFN_HIDDEN = 2816
N_MOD = 6
EPS = 1e-6
NEG_BIG = -1e30
VMEM_LIMIT = 56 * 1024 * 1024


def _cparams(sem):
    return pltpu.CompilerParams(dimension_semantics=sem, vmem_limit_bytes=VMEM_LIMIT)


def _const_spec(shape):
    nd = len(shape)
    return pl.BlockSpec(shape, lambda *_: (0,) * nd, pipeline_mode=pl.Buffered(1))


def _sigmoid(x):
    return 1.0 / (1.0 + jnp.exp(-x))


def _rms(x, g):
    return x * lax.rsqrt(jnp.mean(x * x, axis=-1, keepdims=True) + EPS) * g


def _mod_kernel(c_ref, w_ref, b_ref, o_ref):
    c = c_ref[...]
    s = c * _sigmoid(c)
    o_ref[...] = jnp.dot(s.astype(BF16), w_ref[...].astype(BF16),
                         preferred_element_type=F32) + b_ref[...]


def _mod_call(c_all, w_ada, b_ada):
    rows = c_all.shape[0]
    n_out = w_ada.shape[1]
    tn = 1024
    return pl.pallas_call(
        _mod_kernel,
        grid=(n_out // tn,),
        in_specs=[pl.BlockSpec((rows, D_MODEL), lambda j: (0, 0)),
                  pl.BlockSpec((D_MODEL, tn), lambda j: (0, j)),
                  pl.BlockSpec((1, tn), lambda j: (0, j))],
        out_specs=pl.BlockSpec((rows, tn), lambda j: (0, j)),
        out_shape=jax.ShapeDtypeStruct((rows, n_out), F32),
        compiler_params=_cparams(("arbitrary",)),
        name="adaln_mod",
    )(c_all, w_ada, b_ada)


def _proj_kernel(x_ref, mod_ref, g_ref, w_ref, cos_ref, sin_ref,
                 q1_ref, k1_ref, v1_ref, q4_ref, k4_ref, v4_ref, q16_ref, k16_ref, v16_ref,
                 u_ref, res_ref, tmp_ref, *, tm):
    x = x_ref[...]
    h = _rms(x, g_ref[...]) * (1.0 + mod_ref[1:2, :]) + mod_ref[0:1, :]
    p = jnp.dot(h.astype(BF16), w_ref[...], preferred_element_type=F32)
    cos = jnp.concatenate([cos_ref[...]] * N_PAIRS, axis=1)
    sin = jnp.concatenate([sin_ref[...]] * N_PAIRS, axis=1)
    lane = lax.broadcasted_iota(jnp.int32, (1, ATTN_WIDTH), 1)
    first_half = (lane % HEAD_DIM) < (HEAD_DIM // 2)

    def rope(t):
        fwd = pltpu.roll(t, ATTN_WIDTH - HEAD_DIM // 2, 1)
        bwd = pltpu.roll(t, HEAD_DIM // 2, 1)
        return t * cos + jnp.where(first_half, fwd, bwd) * sin

    qkv = (rope(p[:, 0:ATTN_WIDTH]) * (HEAD_DIM ** -0.5),
           rope(p[:, ATTN_WIDTH:2 * ATTN_WIDTH]),
           p[:, 2 * ATTN_WIDTH:3 * ATTN_WIDTH])
    u_ref[...] = p[:, 3 * ATTN_WIDTH:]
    for a, (t, ref) in enumerate(zip(qkv, (q1_ref, k1_ref, v1_ref))):
        ref[...] = t.astype(BF16)
        for c in range(N_PAIRS):
            res_ref[a * N_PAIRS + c] = t[:, c * PAIR:(c + 1) * PAIR]
    n4, n16 = tm // 4, tm // 16
    for a, (ref4, ref16) in enumerate(((q4_ref, q16_ref), (k4_ref, k16_ref), (v4_ref, v16_ref))):
        for c in range(N_PAIRS):
            slab = a * N_PAIRS + c
            cols = slice(c * PAIR, (c + 1) * PAIR)
            for r4 in range(4):
                t4 = res_ref[slab, pl.ds(r4, n4, stride=4), :]
                ref4[r4, :, cols] = t4.astype(BF16)
                tmp_ref[slab, r4 * n4:(r4 + 1) * n4, :] = t4
            for r4 in range(4):
                for rp in range(4):
                    piece = tmp_ref[slab, pl.ds(r4 * n4 + rp, n16, stride=4), :]
                    ref16[r4 + 4 * rp, :, cols] = piece.astype(BF16)


def _proj_call(x, mod3, norm1_g, w_in, cos_t, sin_t, tm):
    b, s, _ = x.shape
    row = lambda bi, i: (bi, i, 0)
    nat = pl.BlockSpec((None, tm, ATTN_WIDTH), row)
    out_specs, out_shape = [nat] * 3, [jax.ShapeDtypeStruct((b, s, ATTN_WIDTH), BF16)] * 3
    for d in DILATIONS[1:]:
        out_specs += [pl.BlockSpec((None, d, tm // d, ATTN_WIDTH), lambda bi, i: (bi, 0, i, 0))] * 3
        out_shape += [jax.ShapeDtypeStruct((b, d, s // d, ATTN_WIDTH), BF16)] * 3
    out_specs.append(pl.BlockSpec((None, tm, SSM_WIDTH), row))
    out_shape.append(jax.ShapeDtypeStruct((b, s, SSM_WIDTH), F32))
    return pl.pallas_call(
        functools.partial(_proj_kernel, tm=tm),
        grid=(b, s // tm),
        in_specs=[pl.BlockSpec((None, tm, D_MODEL), row),
                  pl.BlockSpec((None, N_MOD, D_MODEL), lambda bi, i: (bi, 0, 0)),
                  _const_spec((1, D_MODEL)),
                  _const_spec((D_MODEL, 4 * ATTN_WIDTH)),
                  pl.BlockSpec((tm, PAIR), lambda bi, i: (i, 0)),
                  pl.BlockSpec((tm, PAIR), lambda bi, i: (i, 0))],
        out_specs=out_specs,
        out_shape=out_shape,
        scratch_shapes=[pltpu.VMEM((3 * N_PAIRS, tm, PAIR), F32)] * 2,
        compiler_params=_cparams(("arbitrary", "arbitrary")),
        name="proj_rope",
    )(x, mod3, norm1_g, w_in, cos_t, sin_t)


def _attn_kernel(bias_ref, q_ref, k_ref, v_ref, o_ref, lse_ref, *, sub_len, tq, rr, pp):
    i = pl.program_id(3)
    lane_b = lax.broadcasted_iota(jnp.int32, (Q_TILE, PAIR), 1) < HEAD_DIM

    def one_residue(r):
        for sb in range(tq // Q_TILE):
            rows = slice(sb * Q_TILE, (sb + 1) * Q_TILE)
            m0 = i * tq + sb * Q_TILE
            start = jnp.clip(m0 - HALF_KEYS, 0, sub_len - K_WIN)
            start = pl.multiple_of(start, HALF_KEYS)
            bias = bias_ref[jnp.where(m0 == 0, 1, jnp.where(m0 == sub_len - Q_TILE, 2, 0))]
            for hp in range(pp):
                cols = slice(hp * PAIR, (hp + 1) * PAIR)
                q = q_ref[r, rows, cols]
                kw = k_ref[r, pl.ds(start, K_WIN), cols]
                vw = v_ref[r, pl.ds(start, K_WIN), cols]
                zero = jnp.zeros_like(q)
                q2 = jnp.concatenate([jnp.where(lane_b, q, zero), jnp.where(lane_b, zero, q)],
                                     axis=0)
                s = lax.dot_general(q2, kw, (((1,), (1,)), ((), ())),
                                    preferred_element_type=F32) + bias
                m = jnp.max(s, axis=1, keepdims=True)
                p = jnp.exp(s - m)
                l = jnp.sum(p, axis=1, keepdims=True)
                pv = jnp.dot(p.astype(BF16), vw, preferred_element_type=F32)
                o2 = pv / l
                lse2 = jnp.broadcast_to(m + jnp.log(l), (2 * Q_TILE, PAIR))
                o_ref[r, rows, cols] = jnp.where(lane_b, o2[:Q_TILE], o2[Q_TILE:]).astype(BF16)
                lse_ref[r, rows, cols] = jnp.where(lane_b, lse2[:Q_TILE], lse2[Q_TILE:])

    if rr == 1:
        one_residue(0)
    else:
        def body(r, carry):
            one_residue(r)
            return carry
        lax.fori_loop(0, rr, body, 0)


def _band_bias():
    row = lax.broadcasted_iota(jnp.int32, (2 * Q_TILE, K_WIN), 0) % Q_TILE
    col = lax.broadcasted_iota(jnp.int32, (2 * Q_TILE, K_WIN), 1)
    offs = jnp.array([-HALF_KEYS, 0, -2 * HALF_KEYS], jnp.int32)[:, None, None]
    valid = jnp.abs(col - row + offs) <= HALF_KEYS
    return jnp.where(valid, 0.0, NEG_BIG).astype(F32)


def _attn_call(qd, kd, vd, bias):
    b, d, sub_len, _ = qd.shape
    pp = max(1, min(N_PAIRS, (8 * 1024 * 1024) // (sub_len * PAIR * 2)))
    tq = min(sub_len, (16 * Q_TILE) // pp)
    rr = max(1, min(d, 16 // (pp * (tq // Q_TILE))))
    qspec = pl.BlockSpec((None, rr, tq, pp * PAIR), lambda bi, r, hp, i: (bi, r, i, hp))
    kspec = pl.BlockSpec((None, rr, sub_len, pp * PAIR), lambda bi, r, hp, i: (bi, r, 0, hp))
    out_sd = lambda dt: jax.ShapeDtypeStruct((b, d, sub_len, ATTN_WIDTH), dt)
    return pl.pallas_call(
        functools.partial(_attn_kernel, sub_len=sub_len, tq=tq, rr=rr, pp=pp),
        grid=(b, d // rr, N_PAIRS // pp, sub_len // tq),
        in_specs=[_const_spec((3, 2 * Q_TILE, K_WIN)), qspec, kspec, kspec],
        out_specs=[qspec, qspec],
        out_shape=[out_sd(BF16), out_sd(F32)],
        compiler_params=_cparams(("arbitrary",) * 4),
        name=f"dilated_attn_d{d}",
    )(bias, qd, kd, vd)


def _ssm_in_kernel(u_ref, t_ref, we_ref, y_ref, e_ref, lhs_ref, *, mc):
    for j in range(CHUNK):
        lhs_ref[:, j * PAIR:(j + 1) * PAIR] = u_ref[pl.ds(j, mc, stride=CHUNK), :].astype(BF16)
    x = lhs_ref[...]
    y_ref[...] = jnp.dot(x, t_ref[...], preferred_element_type=F32)
    e_ref[...] = jnp.dot(x, we_ref[...], preferred_element_type=F32)


def _ssm_in_call(u, t_w, we_w, mc):
    b, s, _ = u.shape
    n_chunks = s // CHUNK
    tile = pl.BlockSpec((None, None, mc, BUNDLE_W), lambda bun, bi, i: (bi, bun, i, 0))
    wspec = pl.BlockSpec((None, BUNDLE_W, BUNDLE_W), lambda bun, bi, i: (bun, 0, 0),
                         pipeline_mode=pl.Buffered(1))
    out_sd = jax.ShapeDtypeStruct((b, N_BUNDLES, n_chunks, BUNDLE_W), F32)
    return pl.pallas_call(
        functools.partial(_ssm_in_kernel, mc=mc),
        grid=(N_BUNDLES, b, n_chunks // mc),
        in_specs=[pl.BlockSpec((None, CHUNK * mc, PAIR), lambda bun, bi, i: (bi, i, bun)),
                  wspec, wspec],
        out_specs=[tile, tile],
        out_shape=[out_sd, out_sd],
        scratch_shapes=[pltpu.VMEM((mc, BUNDLE_W), BF16)],
        compiler_params=_cparams(("arbitrary",) * 3),
        name="s5_chunk_in",
    )(u, t_w, we_w)


def _ssm_scan_kernel(e_ref, a_ref, h_ref, *, n_chunks):
    w = STATE_W
    coef = [(a_ref[dr, 0:1, :], a_ref[dr, 1:2, :]) for dr in range(2)]

    def body(s, carry):
        rows = (s, n_chunks - 1 - s)
        new = []
        for dr in range(2):
            ar, ai = coef[dr]
            hr, hi = carry[2 * dr], carry[2 * dr + 1]
            k = rows[dr]
            h_ref[pl.ds(k, 1), (2 * dr) * w:(2 * dr + 1) * w] = hr
            h_ref[pl.ds(k, 1), (2 * dr + 1) * w:(2 * dr + 2) * w] = hi
            er = e_ref[pl.ds(k, 1), (2 * dr) * w:(2 * dr + 1) * w]
            ei = e_ref[pl.ds(k, 1), (2 * dr + 1) * w:(2 * dr + 2) * w]
            new += [ar * hr - ai * hi + er, ar * hi + ai * hr + ei]
        return tuple(new)

    zero = jnp.zeros((1, w), F32)
    lax.fori_loop(0, n_chunks, body, (zero,) * 4, unroll=4)


def _ssm_scan_call(e, a_pow):
    b, _, n_chunks, _ = e.shape
    tile = pl.BlockSpec((None, None, n_chunks, 4 * STATE_W), lambda bi, bun: (bi, bun, 0, 0))
    return pl.pallas_call(
        functools.partial(_ssm_scan_kernel, n_chunks=n_chunks),
        grid=(b, N_BUNDLES),
        in_specs=[tile,
                  pl.BlockSpec((None, 2, 2, STATE_W), lambda bi, bun: (bun, 0, 0, 0))],
        out_specs=tile,
        out_shape=jax.ShapeDtypeStruct(e.shape, F32),
        compiler_params=_cparams(("arbitrary",) * 2),
        name="s5_chunk_scan",
    )(e, a_pow)


def _ssm_out_kernel(h_ref, v_ref, yi_ref, y_ref, *, mc):
    y = yi_ref[...] + jnp.dot(h_ref[...].astype(BF16), v_ref[...], preferred_element_type=F32)
    for t in range(CHUNK):
        y_ref[pl.ds(t, mc, stride=CHUNK), :] = y[:, t * PAIR:(t + 1) * PAIR]


def _ssm_out_call(h_in, v_w, y_intra, mc):
    b, _, n_chunks, _ = h_in.shape
    tile = pl.BlockSpec((None, None, mc, BUNDLE_W), lambda bun, bi, i: (bi, bun, i, 0))
    wspec = pl.BlockSpec((None, BUNDLE_W, BUNDLE_W), lambda bun, bi, i: (bun, 0, 0),
                         pipeline_mode=pl.Buffered(1))
    return pl.pallas_call(
        functools.partial(_ssm_out_kernel, mc=mc),
        grid=(N_BUNDLES, b, n_chunks // mc),
        in_specs=[tile, wspec, tile],
        out_specs=pl.BlockSpec((None, CHUNK * mc, PAIR), lambda bun, bi, i: (bi, i, bun)),
        out_shape=jax.ShapeDtypeStruct((b, n_chunks * CHUNK, SSM_WIDTH), F32),
        compiler_params=_cparams(("arbitrary",) * 3),
        name="s5_chunk_out",
    )(h_in, v_w, y_intra)


def _cpow(lam_re, lam_im, dt, n):
    nn = n.astype(F32).reshape(n.shape + (1, 1))
    mag = jnp.exp(nn * (lam_re * dt))
    ang = nn * (lam_im * dt)
    return mag * jnp.cos(ang), mag * jnp.sin(ang)


def _expand_kernel(c_ref, e_ref, o_ref, *, tr, row_div, col_div):
    w = jnp.dot(c_ref[...].astype(BF16), e_ref[...], preferred_element_type=F32)
    rows = lax.broadcasted_iota(jnp.int32, (tr, BUNDLE_W), 0) + pl.program_id(1) * tr
    cols = lax.broadcasted_iota(jnp.int32, (tr, BUNDLE_W), 1)
    same_group = (rows // row_div) % BUNDLE_GROUPS == (cols // col_div) % BUNDLE_GROUPS
    o_ref[...] = jnp.where(same_group, w, 0.0).astype(BF16)


def _expand_call(compact, spread, row_div, col_div, name):
    tr = 512
    return pl.pallas_call(
        functools.partial(_expand_kernel, tr=tr, row_div=row_div, col_div=col_div),
        grid=(N_BUNDLES, BUNDLE_W // tr),
        in_specs=[pl.BlockSpec((None, tr, 2 * PAIR), lambda bun, i: (bun, i, 0)),
                  pl.BlockSpec((2 * PAIR, BUNDLE_W), lambda bun, i: (0, 0))],
        out_specs=pl.BlockSpec((None, tr, BUNDLE_W), lambda bun, i: (bun, i, 0)),
        out_shape=jax.ShapeDtypeStruct((N_BUNDLES, BUNDLE_W, BUNDLE_W), BF16),
        compiler_params=_cparams(("arbitrary", "arbitrary")),
        name=name,
    )(compact, spread)


def _expand_readout_kernel(c_ref, e_ref, o_ref):
    rows = lax.broadcasted_iota(jnp.int32, (BUNDLE_W, PAIR), 0)
    cols = lax.broadcasted_iota(jnp.int32, (BUNDLE_W, PAIR), 1)
    same_group = (rows // SSM_STATE) % BUNDLE_GROUPS == cols // SSM_GROUP
    w = jnp.dot(c_ref[...].astype(BF16), e_ref[...], preferred_element_type=F32)
    o_ref[...] = jnp.where(same_group, w, 0.0).astype(BF16)


def _expand_readout_call(compact, spread):
    return pl.pallas_call(
        _expand_readout_kernel,
        grid=(N_BUNDLES,),
        in_specs=[pl.BlockSpec((None, BUNDLE_W, SSM_GROUP), lambda bun: (bun, 0, 0)),
                  pl.BlockSpec((SSM_GROUP, PAIR), lambda bun: (0, 0))],
        out_specs=pl.BlockSpec((None, BUNDLE_W, PAIR), lambda bun: (bun, 0, 0)),
        out_shape=jax.ShapeDtypeStruct((N_BUNDLES, BUNDLE_W, PAIR), BF16),
        compiler_params=_cparams(("arbitrary",)),
        name="s5_expand_readout",
    )(compact, spread)


def _toeplitz_kernel(we_ref, cm_ref, o_ref, lag_ref):
    half = 2 * STATE_W
    span = (CHUNK - 1) * PAIR
    gf = jnp.dot(we_ref[:, 0:half], cm_ref[0:half, :], preferred_element_type=F32)
    gb = jnp.dot(we_ref[:, half:], cm_ref[half:, :], preferred_element_type=F32)
    lag_ref[0:span, :] = gf[0:span]
    lag_ref[span:span + PAIR, :] = gf[span:] + gb[0:PAIR]
    lag_ref[span + PAIR:, :] = gb[PAIR:]
    for t in range(CHUNK):
        first = (CHUNK - 1 - t) * PAIR
        o_ref[:, t * PAIR:(t + 1) * PAIR] = lag_ref[first:first + BUNDLE_W, :].astype(BF16)


def _toeplitz_call(we_w, cm_w):
    return pl.pallas_call(
        _toeplitz_kernel,
        grid=(N_BUNDLES,),
        in_specs=[pl.BlockSpec((None, BUNDLE_W, BUNDLE_W), lambda bun: (bun, 0, 0)),
                  pl.BlockSpec((None, BUNDLE_W, PAIR), lambda bun: (bun, 0, 0))],
        out_specs=pl.BlockSpec((None, BUNDLE_W, BUNDLE_W), lambda bun: (bun, 0, 0)),
        out_shape=jax.ShapeDtypeStruct((N_BUNDLES, BUNDLE_W, BUNDLE_W), BF16),
        scratch_shapes=[pltpu.VMEM(((2 * CHUNK - 1) * PAIR, PAIR), F32)],
        compiler_params=_cparams(("arbitrary",)),
        name="s5_toeplitz",
    )(we_w, cm_w)


def _ssm_weights(lam_re, lam_im, log_dt, b_re, b_im, c_re, c_im):
    steps = jnp.arange(CHUNK)
    readout, w_e, v_c, a_pow = [], [], [], []
    for dr in range(2):
        lr, li = lam_re[dr], lam_im[dr]
        dt = jnp.exp(log_dt[dr])[:, None]
        a_r, a_i = _cpow(lr, li, dt, jnp.ones((), F32))
        nr = a_r - 1.0
        den = lr * lr + li * li
        z_r = ((nr * lr + a_i * li) / den)[..., None]
        z_i = ((a_i * lr - nr * li) / den)[..., None]
        bb_r = z_r * b_re[dr] - z_i * b_im[dr]
        bb_i = z_r * b_im[dr] + z_i * b_re[dr]
        cr = jnp.swapaxes(c_re[dr], 1, 2)
        ci = jnp.swapaxes(c_im[dr], 1, 2)

        readout.append((cr, -ci))

        e_r, e_i = _cpow(lr, li, dt, (CHUNK - 1 - steps) if dr == 0 else steps)
        we_r = e_r[..., None] * bb_r - e_i[..., None] * bb_i
        we_i = e_r[..., None] * bb_i + e_i[..., None] * bb_r
        w_e.append((we_r, we_i))

        o_r, o_i = _cpow(lr, li, dt, (steps + 1) if dr == 0 else (CHUNK - steps))
        vo_r = o_r[..., None] * cr - o_i[..., None] * ci
        vo_i = o_r[..., None] * ci + o_i[..., None] * cr
        v_c.append((vo_r, -vo_i))

        a_pow.append(_cpow(lr, li, dt, jnp.full((), CHUNK, F32)))

    def bundle(x):
        return x.reshape((x.shape[0], N_BUNDLES, BUNDLE_GROUPS) + x.shape[2:])

    cm = jnp.stack([jnp.stack(readout[dr], axis=0) for dr in range(2)], axis=0)
    cm = cm.reshape(2, 2, N_BUNDLES, BUNDLE_GROUPS, SSM_STATE, SSM_GROUP)
    cm_c = jnp.transpose(cm, (2, 0, 1, 3, 4, 5)).reshape(N_BUNDLES, BUNDLE_W, SSM_GROUP)

    quarters = [w_e[0][0], w_e[0][1], w_e[1][0], w_e[1][1]]
    we = jnp.stack([bundle(x) for x in quarters], axis=0)
    we_c = jnp.transpose(we, (2, 1, 3, 5, 0, 4)).reshape(N_BUNDLES, BUNDLE_W, 2 * PAIR)

    quarters = [v_c[0][0], v_c[0][1], v_c[1][0], v_c[1][1]]
    vv = jnp.stack([bundle(x) for x in quarters], axis=0)
    v_cmp = jnp.transpose(vv, (2, 0, 3, 4, 1, 5)).reshape(N_BUNDLES, BUNDLE_W, 2 * PAIR)

    src = jnp.arange(2 * PAIR)[:, None]
    dst = jnp.arange(BUNDLE_W)[None, :]
    spread_tc = ((src // SSM_GROUP == dst // PAIR) & (src % SSM_GROUP == dst % SSM_GROUP)).astype(BF16)
    spread_qp = ((src // SSM_STATE == dst // STATE_W) & (src % SSM_STATE == dst % SSM_STATE)).astype(BF16)

    spread_c = (jnp.arange(SSM_GROUP)[:, None] == jnp.arange(PAIR)[None, :] % SSM_GROUP).astype(BF16)
    we_w = _expand_call(we_c, spread_qp, SSM_GROUP, SSM_STATE, "s5_expand_inject")
    t_w = _toeplitz_call(we_w, _expand_readout_call(cm_c, spread_c))
    v_w = _expand_call(v_cmp, spread_tc, SSM_STATE, SSM_GROUP, "s5_expand_carry")

    ap = jnp.stack([jnp.stack([a_pow[dr][0], a_pow[dr][1]], axis=0) for dr in range(2)], axis=0)
    ap = ap.reshape(2, 2, N_BUNDLES, STATE_W)
    ap = jnp.transpose(ap, (2, 0, 1, 3))
    return t_w, we_w, v_w, ap


def _mix_kernel(x_ref, mod_ref, o1_ref, l1_ref, o4_ref, l4_ref, o16_ref, l16_ref,
                y_ref, u_ref, dskip_ref, wglu_ref, bglu_ref, ang_ref, sng_ref,
                wo_ref, x1_ref, nat_ref, tmp_ref, *, tm):
    n4, n16 = tm // 4, tm // 16

    def natural(ref, d, base):
        for c in range(N_PAIRS):
            cols = slice(c * PAIR, (c + 1) * PAIR)
            for r4 in range(4):
                if d == 16:
                    for rp in range(4):
                        tmp_ref[c, pl.ds(r4 * n4 + rp, n16, stride=4), :] = (
                            ref[r4 + 4 * rp, :, cols].astype(F32))
                    quarter = tmp_ref[c, r4 * n4:(r4 + 1) * n4, :]
                else:
                    quarter = ref[r4, :, cols].astype(F32)
                nat_ref[base + c, pl.ds(r4, n4, stride=4), :] = quarter
        return jnp.concatenate([nat_ref[base + c] for c in range(N_PAIRS)], axis=1)

    o2, l2 = natural(o4_ref, 4, 0), natural(l4_ref, 4, N_PAIRS)
    o3, l3 = natural(o16_ref, 16, 2 * N_PAIRS), natural(l16_ref, 16, 3 * N_PAIRS)
    l1 = l1_ref[...]
    lm = jnp.maximum(jnp.maximum(l1, l2), l3)
    e1, e2, e3 = jnp.exp(l1 - lm), jnp.exp(l2 - lm), jnp.exp(l3 - lm)
    attn = (e1 * o1_ref[...].astype(F32) + e2 * o2 + e3 * o3) / (e1 + e2 + e3)
    an = _rms(attn, ang_ref[...])

    y = y_ref[...] + dskip_ref[...] * u_ref[...]
    g = 0.5 * y * (1.0 + jnp.tanh(math.sqrt(2.0 / math.pi) * (y + 0.044715 * (y * y * y))))
    z = jnp.dot(g.astype(BF16), wglu_ref[...], preferred_element_type=F32) + bglu_ref[...]
    sn = _rms(g * _sigmoid(z), sng_ref[...])

    mixed = (jnp.dot(an.astype(BF16), wo_ref[0:ATTN_WIDTH, :], preferred_element_type=F32)
             + jnp.dot(sn.astype(BF16), wo_ref[ATTN_WIDTH:, :], preferred_element_type=F32))
    x1_ref[...] = x_ref[...] + mod_ref[2:3, :] * mixed


def _mix_call(x, mod3, branch_outs, y_ssm, u, d_skip, w_glu, b_glu, attn_g, ssm_g, w_o, tm):
    b, s, _ = x.shape
    row = lambda bi, i: (bi, i, 0)
    wide = pl.BlockSpec((None, tm, D_MODEL), row)
    half = pl.BlockSpec((None, tm, ATTN_WIDTH), row)
    res = lambda d: pl.BlockSpec((None, d, tm // d, ATTN_WIDTH), lambda bi, i: (bi, 0, i, 0))
    return pl.pallas_call(
        functools.partial(_mix_kernel, tm=tm),
        grid=(b, s // tm),
        in_specs=[wide, pl.BlockSpec((None, N_MOD, D_MODEL), lambda bi, i: (bi, 0, 0)),
                  half, half, res(4), res(4), res(16), res(16), half, half,
                  _const_spec((1, SSM_WIDTH)), _const_spec((SSM_WIDTH, SSM_WIDTH)),
                  _const_spec((1, SSM_WIDTH)), _const_spec((1, ATTN_WIDTH)),
                  _const_spec((1, SSM_WIDTH)), _const_spec((D_MODEL, D_MODEL))],
        out_specs=wide,
        out_shape=jax.ShapeDtypeStruct((b, s, D_MODEL), F32),
        scratch_shapes=[pltpu.VMEM((4 * N_PAIRS, tm, PAIR), F32),
                        pltpu.VMEM((N_PAIRS, tm, PAIR), F32)],
        compiler_params=_cparams(("arbitrary", "arbitrary")),
        name="merge_mix",
    )(x, mod3, *branch_outs, y_ssm, u, d_skip, w_glu, b_glu, attn_g, ssm_g, w_o)


FFN_CHUNK = 256


def _ffn_kernel(x1_ref, mod_ref, n2g_ref, w1_ref, w3_ref, w2_ref, fg_ref, out_ref, acc_ref):
    x1 = x1_ref[...]
    h = (_rms(x1, n2g_ref[...]) * (1.0 + mod_ref[4:5, :]) + mod_ref[3:4, :]).astype(BF16)
    for c in range(FFN_HIDDEN // FFN_CHUNK):
        cols = slice(c * FFN_CHUNK, (c + 1) * FFN_CHUNK)
        a = jnp.dot(h, w1_ref[:, cols], preferred_element_type=F32)
        bgate = jnp.dot(h, w3_ref[:, cols], preferred_element_type=F32)
        hid = (a * _sigmoid(a) * bgate).astype(BF16)
        part = jnp.dot(hid, w2_ref[cols, :], preferred_element_type=F32)
        if c == 0:
            acc_ref[...] = part
        else:
            acc_ref[...] += part
    x2 = x1 + mod_ref[5:6, :] * acc_ref[...]
    out_ref[...] = _rms(x2, fg_ref[...])


def _ffn_call(x1, mod3, norm2_g, w1, w3, w2, final_g, tm):
    b, s, _ = x1.shape
    wide = pl.BlockSpec((None, tm, D_MODEL), lambda bi, i: (bi, i, 0))
    return pl.pallas_call(
        _ffn_kernel,
        grid=(b, s // tm),
        in_specs=[wide, pl.BlockSpec((None, N_MOD, D_MODEL), lambda bi, i: (bi, 0, 0)),
                  _const_spec((1, D_MODEL)), _const_spec((D_MODEL, FFN_HIDDEN)),
                  _const_spec((D_MODEL, FFN_HIDDEN)), _const_spec((FFN_HIDDEN, D_MODEL)),
                  _const_spec((1, D_MODEL))],
        out_specs=wide,
        out_shape=jax.ShapeDtypeStruct((b, s, D_MODEL), F32),
        scratch_shapes=[pltpu.VMEM((tm, D_MODEL), F32)],
        compiler_params=_cparams(("arbitrary", "arbitrary")),
        name="swiglu_ffn",
    )(x1, mod3, norm2_g, w1, w3, w2, final_g)


def _rope_tables(seq_len):
    lane = jnp.arange(PAIR)
    inv = 1.0 / (ROPE_THETA ** ((2 * (lane % (HEAD_DIM // 2))).astype(F32) / HEAD_DIM))
    sign = jnp.where(lane % HEAD_DIM < HEAD_DIM // 2, -1.0, 1.0).astype(F32)
    ang = jnp.arange(seq_len, dtype=F32)[:, None] * inv[None, :]
    return jnp.cos(ang), jnp.sin(ang) * sign[None, :]


def _encode(x, mod3, p):
    b, s, _ = x.shape
    n_chunks = s // CHUNK
    cos_t, sin_t = p["rope"][0][:s], p["rope"][1][:s]
    (q1, k1, v1, q4, k4, v4, q16, k16, v16, u) = _proj_call(
        x, mod3, p["norm1_g"], p["w_in"], cos_t, sin_t, tm=512)

    bias = _band_bias()
    o1, l1 = _attn_call(q1[:, None], k1[:, None], v1[:, None], bias)
    o4, l4 = _attn_call(q4, k4, v4, bias)
    o16, l16 = _attn_call(q16, k16, v16, bias)
    branch_outs = (o1[:, 0], l1[:, 0], o4, l4, o16, l16)

    mc = min(512, n_chunks)
    y_intra, e = _ssm_in_call(u, p["t_w"], p["we_w"], mc)
    h_in = _ssm_scan_call(e, p["a_pow"])
    y_ssm = _ssm_out_call(h_in, p["v_w"], y_intra, mc)

    x1 = _mix_call(x, mod3, branch_outs, y_ssm, u, p["d_skip"], p["w_glu"], p["b_glu"],
                   p["attn_norm_g"], p["ssm_norm_g"], p["w_o"], tm=512)
    return _ffn_call(x1, mod3, p["norm2_g"], p["w1"], p["w3"], p["w2"], p["final_g"], tm=1024)


def kernel(x_prompt, x_sample, c_prompt, c_sample, w_ada, b_ada, norm1_g, w_in, lam_re, lam_im,
           log_dt, b_re, b_im, c_re, c_im, d_skip, w_glu, b_glu, attn_norm_g, ssm_norm_g, w_o,
           norm2_g, w1, w3, w2, final_g):
    nb_p, nb_s = c_prompt.shape[0], c_sample.shape[0]
    rows = -(-(nb_p + nb_s) // 8) * 8
    c_all = jnp.concatenate([c_prompt, c_sample,
                             jnp.zeros((rows - nb_p - nb_s, D_MODEL), F32)], axis=0)
    mod = _mod_call(c_all, w_ada[0], b_ada[0][None, :]).reshape(rows, N_MOD, D_MODEL)

    t_w, we_w, v_w, a_pow = _ssm_weights(lam_re[0], lam_im[0], log_dt[0], b_re[0], b_im[0],
                                         c_re[0], c_im[0])
    p = dict(norm1_g=norm1_g[0][None, :], w_in=w_in[0].astype(BF16),
             t_w=t_w, we_w=we_w, v_w=v_w, a_pow=a_pow,
             d_skip=d_skip[0][None, :], w_glu=w_glu[0].astype(BF16), b_glu=b_glu[0][None, :],
             attn_norm_g=attn_norm_g[0][None, :], ssm_norm_g=ssm_norm_g[0][None, :],
             w_o=w_o[0].astype(BF16), norm2_g=norm2_g[0][None, :],
             w1=w1[0].astype(BF16), w3=w3[0].astype(BF16), w2=w2[0].astype(BF16),
             final_g=final_g[None, :])
    p["rope"] = _rope_tables(max(x_prompt.shape[1], x_sample.shape[1]))
    y_prompt = _encode(x_prompt, mod[:nb_p], p)
    y_sample = _encode(x_sample, mod[nb_p:nb_p + nb_s], p)
    return (y_prompt, y_sample)
```
